```python
import jax, jax.numpy as jnp
from jax import lax
import numpy as np

D_MODEL = 1024
BATCH = 8
SEQ = 2048
DEPTH = 4
DEC_BATCH = 128
DEC_SEQ = 4
PAST_LEN = 16384
PAGE_SIZE = 128

N_MIXERS = 2
N_POOL_LAYERS = (DEPTH + N_MIXERS - 1) // N_MIXERS
N_LRU_LAYERS = DEPTH // N_MIXERS
POOL_GROUPS = 4
POOL_WINDOWS = (2, 4, 8, 16)
POOL_GW = D_MODEL // POOL_GROUPS
POOL_BUF = max(POOL_WINDOWS) - 1
LRU_WIDTH = D_MODEL
LRU_BLOCKS = 4
LRU_BW = LRU_WIDTH // LRU_BLOCKS
CONV_WIDTH = 4
LRU_C = 8.0
N_EXPERTS = 16
N_EXPERT_GROUPS = 4
EXPERTS_PER_GROUP = N_EXPERTS // N_EXPERT_GROUPS
TOP_K = 2
D_FF = 512
EPS = 1e-6

kernel_name = "hybrid_pool_rglru_shared_router_moe_adaln_step"


def rmsnorm(x, g):
    x32 = x.astype(jnp.float32)
    y = x32 * lax.rsqrt(jnp.mean(x32 * x32, axis=-1, keepdims=True) + EPS)
    return (y * g.astype(jnp.float32)).astype(x.dtype)


def modulate(x, g, shift, scale):
    return rmsnorm(x, g) * (1 + scale[:, None, :]) + shift[:, None, :]


def pool_mixer(h, buf, start, w_pool, scale):
    B, S, D = h.shape
    xp = jnp.concatenate([buf.astype(h.dtype), h], axis=1)
    xp32 = xp.astype(jnp.float32)
    cs = jnp.concatenate([jnp.zeros((B, 1, D), jnp.float32), jnp.cumsum(xp32, axis=1)], axis=1)
    off = POOL_BUF
    pos = start + jnp.arange(S)
    end = cs[:, off + 1: off + 1 + S]
    parts = []
    for g, w in enumerate(POOL_WINDOWS):
        sl = slice(g * POOL_GW, (g + 1) * POOL_GW)
        begin = cs[:, off + 1 - w: off + 1 - w + S, sl]
        cnt = jnp.minimum(pos + 1, w).astype(jnp.float32)[None, :, None]
        parts.append((end[..., sl] - begin) / cnt)
    pooled = jnp.concatenate(parts, axis=-1) - xp32[:, off:]
    pooled = pooled.astype(h.dtype).reshape(B, S, POOL_GROUPS, POOL_GW)
    mixed = jnp.einsum('bsgc,gcd->bsgd', pooled, w_pool).reshape(B, S, D)
    new_buf = xp[:, -POOL_BUF:]
    return mixed * scale, new_buf


def _lin_comb(left, right):
    al, bl = left
    ar, br = right
    return (al * ar, ar * bl + br)


def recurrent_block(h, conv_buf, h0, start, w_x, b_x, w_y, b_y, cw, cb, w_a, b_a, w_i, b_i, lam, w_out, b_out):
    B, S, _ = h.shape
    xb = h @ w_x + b_x
    yb = jax.nn.gelu(h @ w_y + b_y)
    xp = jnp.concatenate([conv_buf.astype(xb.dtype), xb], axis=1)
    xc = xp[:, 0:S] * cw[0]
    for k in range(1, CONV_WIDTH):
        xc = xc + xp[:, k:k + S] * cw[k]
    xc = xc + cb
    new_conv = xp[:, -(CONV_WIDTH - 1):]
    xblk = xc.reshape(B, S, LRU_BLOCKS, LRU_BW)
    r = jax.nn.sigmoid((jnp.einsum('bsnc,ncd->bsnd', xblk, w_a).reshape(B, S, LRU_WIDTH) + b_a).astype(jnp.float32))
    i = jax.nn.sigmoid((jnp.einsum('bsnc,ncd->bsnd', xblk, w_i).reshape(B, S, LRU_WIDTH) + b_i).astype(jnp.float32))
    log_a = -LRU_C * r * jax.nn.softplus(-lam.astype(jnp.float32))
    a = jnp.exp(log_a)
    mult = jnp.sqrt(jnp.maximum(1.0 - jnp.exp(2.0 * log_a), 0.0))
    pos = start + jnp.arange(S)
    mult = jnp.where((pos == 0)[None, :, None], 1.0, mult)
    b = xc.astype(jnp.float32) * i * mult
    b = b.at[:, 0].add(a[:, 0] * h0.astype(jnp.float32))
    _, hs = lax.associative_scan(_lin_comb, (a, b), axis=1)
    y = (hs.astype(h.dtype) * yb) @ w_out + b_out
    return y, new_conv, hs[:, -1].astype(h.dtype)


def moe(h, router_w, router_b, w_gate, w_up, w_down):
    B, S, D = h.shape
    t = h.reshape(-1, D)
    logits = t.astype(jnp.float32) @ router_w.astype(jnp.float32)
    scores = jax.nn.softmax(logits, axis=-1)
    sel = scores + router_b.astype(jnp.float32)
    gscore = lax.top_k(sel.reshape(-1, N_EXPERT_GROUPS, EXPERTS_PER_GROUP), TOP_K)[0].sum(-1)
    best = jnp.argmax(gscore, axis=-1)
    expert_group = jnp.arange(N_EXPERTS) // EXPERTS_PER_GROUP
    masked = jnp.where(expert_group[None, :] == best[:, None], sel, -jnp.inf)
    _, idx = lax.top_k(masked, TOP_K)
    wts = jnp.take_along_axis(scores, idx, axis=-1)
    wts = wts / jnp.sum(wts, axis=-1, keepdims=True)
    gates = jnp.sum(jax.nn.one_hot(idx, N_EXPERTS, dtype=jnp.float32) * wts[..., None], axis=1)
    hg = jnp.einsum('td,edf->tef', t, w_gate)
    hu = jnp.einsum('td,edf->tef', t, w_up)
    hid = jax.nn.silu(hg) * hu * gates[..., None].astype(t.dtype)
    out = jnp.einsum('tef,efd->td', hid, w_down)
    return out.reshape(B, S, D)


def trunk(x, c, start, pool_bufs, conv_bufs, lru_hs, w_mod, b_mod, norm_g, pool_w, pool_scale,
          lru_w_x, lru_b_x, lru_w_y, lru_b_y, conv_w, conv_b, lru_w_a, lru_b_a, lru_w_i, lru_b_i,
          lru_lambda, lru_w_out, lru_b_out, router_w, router_b, moe_w_gate, moe_w_up, moe_w_down, final_g):
    new_pool, new_conv, new_h = [], [], []
    cs = jax.nn.silu(c)
    for i in range(DEPTH):
        mod = cs @ w_mod[i] + b_mod[i]
        sh1, sc1, g1, sh2, sc2, g2 = jnp.split(mod, 6, axis=-1)
        h = modulate(x, norm_g[i, 0], sh1, sc1)
        j = i // N_MIXERS
        if i % N_MIXERS == 0:
            mix, nb = pool_mixer(h, pool_bufs[j], start, pool_w[j], pool_scale[j])
            new_pool.append(nb)
        else:
            mix, nc, nh = recurrent_block(h, conv_bufs[j], lru_hs[j], start,
                                          lru_w_x[j], lru_b_x[j], lru_w_y[j], lru_b_y[j],
                                          conv_w[j], conv_b[j], lru_w_a[j], lru_b_a[j],
                                          lru_w_i[j], lru_b_i[j], lru_lambda[j],
                                          lru_w_out[j], lru_b_out[j])
            new_conv.append(nc)
            new_h.append(nh)
        x = x + g1[:, None, :] * mix
        h2 = modulate(x, norm_g[i, 1], sh2, sc2)
        x = x + g2[:, None, :] * moe(h2, router_w, router_b, moe_w_gate[i], moe_w_up[i], moe_w_down[i])
    y = rmsnorm(x, final_g)
    return y, jnp.stack(new_pool), jnp.stack(new_conv), jnp.stack(new_h)


def setup_inputs(seed: int = 0) -> dict:
    key = jax.random.key(seed)
    ks = jax.random.split(key, 32)
    f = jnp.float32
    D, W, E, F = D_MODEL, LRU_WIDTH, N_EXPERTS, D_FF
    nrm = lambda k, s, sc: jax.random.normal(k, s, f) * sc
    u = jax.random.uniform(ks[24], (N_LRU_LAYERS, W), f, 0.9, 0.999)
    p = u ** (1.0 / LRU_C)
    lam = jnp.log(p) - jnp.log1p(-p)
    return {
        "x_prompt": nrm(ks[0], (BATCH, SEQ, D), 1.0),
        "x_sample": nrm(ks[1], (DEC_BATCH, DEC_SEQ, D), 1.0),
        "c_prompt": nrm(ks[2], (BATCH, D), 1.0),
        "c_sample": nrm(ks[3], (DEC_BATCH, D), 1.0),
        "state_pool_buf": nrm(ks[4], (N_POOL_LAYERS, DEC_BATCH, POOL_BUF, D), 1.0),
        "state_conv_buf": nrm(ks[5], (N_LRU_LAYERS, DEC_BATCH, CONV_WIDTH - 1, W), 1.0),
        "state_lru_h": nrm(ks[6], (N_LRU_LAYERS, DEC_BATCH, W), 0.5),
        "w_mod": nrm(ks[7], (DEPTH, D, 6 * D), 0.5 * D ** -0.5),
        "b_mod": nrm(ks[8], (DEPTH, 6 * D), 0.02),
        "norm_g": 1.0 + nrm(ks[9], (DEPTH, 2, D), 0.1),
        "pool_w": nrm(ks[10], (N_POOL_LAYERS, POOL_GROUPS, POOL_GW, POOL_GW), POOL_GW ** -0.5),
        "pool_scale": 1.0 + nrm(ks[11], (N_POOL_LAYERS, D), 0.1),
        "lru_w_x": nrm(ks[12], (N_LRU_LAYERS, D, W), D ** -0.5),
        "lru_b_x": nrm(ks[13], (N_LRU_LAYERS, W), 0.02),
        "lru_w_y": nrm(ks[14], (N_LRU_LAYERS, D, W), D ** -0.5),
        "lru_b_y": nrm(ks[15], (N_LRU_LAYERS, W), 0.02),
        "conv_w": nrm(ks[16], (N_LRU_LAYERS, CONV_WIDTH, W), CONV_WIDTH ** -0.5),
        "conv_b": nrm(ks[17], (N_LRU_LAYERS, W), 0.02),
        "lru_w_a": nrm(ks[18], (N_LRU_LAYERS, LRU_BLOCKS, LRU_BW, LRU_BW), LRU_BW ** -0.5),
        "lru_b_a": nrm(ks[19], (N_LRU_LAYERS, W), 0.02),
        "lru_w_i": nrm(ks[20], (N_LRU_LAYERS, LRU_BLOCKS, LRU_BW, LRU_BW), LRU_BW ** -0.5),
        "lru_b_i": nrm(ks[21], (N_LRU_LAYERS, W), 0.02),
        "lru_lambda": lam,
        "lru_w_out": nrm(ks[22], (N_LRU_LAYERS, W, D), W ** -0.5),
        "lru_b_out": nrm(ks[23], (N_LRU_LAYERS, D), 0.02),
        "router_w": nrm(ks[25], (D, E), D ** -0.5),
        "router_b": nrm(ks[26], (E,), 0.01),
        "moe_w_gate": nrm(ks[27], (DEPTH, E, D, F), D ** -0.5),
        "moe_w_up": nrm(ks[28], (DEPTH, E, D, F), D ** -0.5),
        "moe_w_down": nrm(ks[29], (DEPTH, E, F, D), F ** -0.5),
        "final_g": 1.0 + nrm(ks[30], (D,), 0.1),
    }


def reference(x_prompt, x_sample, c_prompt, c_sample, state_pool_buf, state_conv_buf, state_lru_h,
              w_mod, b_mod, norm_g, pool_w, pool_scale, lru_w_x, lru_b_x, lru_w_y, lru_b_y,
              conv_w, conv_b, lru_w_a, lru_b_a, lru_w_i, lru_b_i, lru_lambda, lru_w_out, lru_b_out,
              router_w, router_b, moe_w_gate, moe_w_up, moe_w_down, final_g):
    bp = x_prompt.shape[0]
    dt = x_prompt.dtype
    pool0 = jnp.zeros((N_POOL_LAYERS, bp, POOL_BUF, D_MODEL), dt)
    conv0 = jnp.zeros((N_LRU_LAYERS, bp, CONV_WIDTH - 1, LRU_WIDTH), dt)
    h0 = jnp.zeros((N_LRU_LAYERS, bp, LRU_WIDTH), dt)
    y_prompt, new_pool_prompt, new_conv_prompt, new_h_prompt = trunk(
        x_prompt, c_prompt, 0, pool0, conv0, h0, w_mod, b_mod, norm_g, pool_w, pool_scale,
        lru_w_x, lru_b_x, lru_w_y, lru_b_y, conv_w, conv_b, lru_w_a, lru_b_a, lru_w_i, lru_b_i,
        lru_lambda, lru_w_out, lru_b_out, router_w, router_b, moe_w_gate, moe_w_up, moe_w_down, final_g)
    y_sample, new_pool_sample, new_conv_sample, new_h_sample = trunk(
        x_sample, c_sample, PAST_LEN, state_pool_buf, state_conv_buf, state_lru_h,
        w_mod, b_mod, norm_g, pool_w, pool_scale,
        lru_w_x, lru_b_x, lru_w_y, lru_b_y, conv_w, conv_b, lru_w_a, lru_b_a, lru_w_i, lru_b_i,
        lru_lambda, lru_w_out, lru_b_out, router_w, router_b, moe_w_gate, moe_w_up, moe_w_down, final_g)
    return (y_prompt, y_sample, new_pool_prompt, new_pool_sample, new_conv_prompt, new_conv_sample, new_h_prompt, new_h_sample)
```

```python
import functools

import jax
import jax.numpy as jnp
from jax import lax
from jax.experimental import pallas as pl
from jax.experimental.pallas import tpu as pltpu

D = 1024
DEPTH = 4
PAST_LEN = 16384
POOL_WINDOWS = (2, 4, 8, 16)
POOL_GW = 256
POOL_BUF = 15
LRU_BLOCKS = 4
LRU_BW = 256
CONV_WIDTH = 4
LRU_C = 8.0
N_EXPERTS = 16
GROUP_SIZE = 4
D_FF = 512
EPS = 1e-6

MOD_ROWS = 128
TM = 512
TM_MIX = 256
TMM = 256
GATE_LANES = 128
DX = D + GATE_LANES
N_BUCKETS = 24
BUCKET_ROWS = 32
VMEM_LIMIT = 56 * 1024 * 1024

_PAIR_A = (0, 0, 0, 1, 1, 2)
_PAIR_B = (1, 2, 3, 2, 3, 3)


def _bf(x):
    return x.astype(jnp.bfloat16)


def _split(x):
    hi = _bf(x)
    lo = _bf(x - hi.astype(jnp.float32))
    return hi, lo


def _split_weight(w, axis=0):
    return jnp.stack(_split(w), axis=axis)


def _dot(a, b):
    return jnp.dot(a, b, preferred_element_type=jnp.float32)


def _dot3(a_hi, a_lo, w_hi, w_lo):
    rows = a_hi.shape[0]
    both = _dot(jnp.concatenate([a_hi, a_lo], axis=0), w_hi)
    return both[0:rows] + both[rows:2 * rows] + _dot(a_hi, w_lo)


def _dot_nt(a, b):
    return lax.dot_general(a, b, (((1,), (1,)), ((), ())), preferred_element_type=jnp.float32)


def _mod_kernel(c_ref, w_ref, b_ref, o_ref):
    c = c_ref[...]
    cs = c * jax.nn.sigmoid(c)
    c_hi, c_lo = _split(cs)
    w_hi, w_lo = _split(w_ref[...])
    o_ref[...] = _dot(c_hi, w_hi) + _dot(c_lo, w_hi) + _dot(c_hi, w_lo) + b_ref[...]


def _modulation(c_exp, w_mod, b_mod):
    rows = c_exp.shape[0]
    return pl.pallas_call(
        _mod_kernel,
        out_shape=jax.ShapeDtypeStruct((DEPTH, 6, rows, D), jnp.float32),
        grid=(DEPTH, 6),
        in_specs=[
            pl.BlockSpec((rows, D), lambda i, n: (0, 0)),
            pl.BlockSpec((None, D, D), lambda i, n: (i, 0, n)),
            pl.BlockSpec((None, None, 1, D), lambda i, n: (i, n, 0, 0)),
        ],
        out_specs=pl.BlockSpec((None, None, rows, D), lambda i, n: (i, n, 0, 0)),
        compiler_params=pltpu.CompilerParams(
            dimension_semantics=("arbitrary", "arbitrary"), vmem_limit_bytes=VMEM_LIMIT),
        name="modulation",
    )(c_exp, w_mod, b_mod.reshape(DEPTH, 6, 1, D))


def _norm_mod(x, gamma, shift, scale):
    rows = x.shape[0]
    ms = jnp.mean(x * x, axis=-1, keepdims=True)
    y = x * lax.rsqrt(ms + EPS) * gamma
    y3 = y.reshape(rows // MOD_ROWS, MOD_ROWS, D)
    return (y3 * (1.0 + scale)[None] + shift[None]).reshape(rows, D)


def _gated_add(x, gate, y):
    rows = x.shape[0]
    y3 = y.reshape(rows // MOD_ROWS, MOD_ROWS, D)
    return x + (y3 * gate[None]).reshape(rows, D)


def _route(h2, rw_ref, rb_ref, run_ref):
    rows = h2.shape[0]
    h_hi, h_lo = _split(h2)
    w_hi, w_lo = _split(rw_ref[...])
    logits = _dot_nt(w_hi, h_hi) + _dot_nt(w_lo, h_hi) + _dot_nt(w_hi, h_lo)
    m = jnp.max(logits, axis=0, keepdims=True)
    e = jnp.exp(logits - m)
    scores = e / jnp.sum(e, axis=0, keepdims=True)
    sel = scores + rb_ref[...]
    eidx = lax.broadcasted_iota(jnp.int32, (N_EXPERTS, rows), 0).astype(jnp.float32)
    gidx = jnp.floor(eidx * (1.0 / GROUP_SIZE))
    neg = -jnp.inf
    sentinel = float(N_EXPERTS)

    def top2(vals):
        v1 = jnp.max(vals, axis=0, keepdims=True)
        i1 = jnp.min(jnp.where(vals == v1, eidx, sentinel), axis=0, keepdims=True)
        rest = jnp.where(eidx == i1, neg, vals)
        v2 = jnp.max(rest, axis=0, keepdims=True)
        i2 = jnp.min(jnp.where(rest == v2, eidx, sentinel), axis=0, keepdims=True)
        return v1, i1, v2, i2

    best = jnp.zeros((1, rows), jnp.float32)
    best_v = None
    for g in range(N_EXPERTS // GROUP_SIZE):
        v1, _, v2, _ = top2(jnp.where(gidx == float(g), sel, neg))
        gs = v1 + v2
        if best_v is None:
            best_v = gs
        else:
            upd = gs > best_v
            best = jnp.where(upd, float(g), best)
            best_v = jnp.where(upd, gs, best_v)
    _, i1, _, i2 = top2(jnp.where(gidx == best, sel, neg))
    chosen = (eidx == i1) | (eidx == i2)
    ssum = jnp.sum(jnp.where(chosen, scores, 0.0), axis=0, keepdims=True)
    gates = jnp.where(chosen, scores / ssum, 0.0)
    ea = jnp.minimum(i1, i2)
    eb = jnp.maximum(i1, i2)
    wa = jnp.sum(jnp.where(eidx == ea, gates, 0.0), axis=0, keepdims=True)
    wb = jnp.sum(jnp.where(eidx == eb, gates, 0.0), axis=0, keepdims=True)
    a = ea - GROUP_SIZE * best
    b = eb - GROUP_SIZE * best
    pair = jnp.where(a == 0.0, b - 1.0, jnp.where(a == 1.0, b + 1.0, 5.0))
    bucket = 6.0 * best + pair

    bidx = lax.broadcasted_iota(jnp.int32, (BUCKET_ROWS, rows), 0).astype(jnp.float32)
    onehot = jnp.where(bidx == bucket, 1.0, 0.0)
    src = lax.broadcasted_iota(jnp.int32, (rows, rows), 0)
    dst = lax.broadcasted_iota(jnp.int32, (rows, rows), 1)
    before = _bf(jnp.where(src < dst, 1.0, 0.0))
    earlier = _dot(_bf(onehot), before)
    base = run_ref[:, 0:1]
    rank = jnp.sum(onehot * (earlier + base), axis=0, keepdims=True)
    run_ref[...] = run_ref[...] + jnp.sum(onehot, axis=1, keepdims=True)
    return bucket, rank, wa, wb


def _emit_routed(h2, bucket, rank, wa, wb, h2_ref, route_ref, cnt_ref, run_ref):
    rows = h2.shape[0]
    lane_row = lax.broadcasted_iota(jnp.int32, (GATE_LANES, rows), 0)
    gate_t = jnp.where(lane_row == 0, wa, jnp.where(lane_row == 1, wb, 0.0))
    h2_ref[:, :, 0:D] = h2.reshape(rows, 1, D)
    h2_ref[:, :, D:DX] = gate_t.T.reshape(rows, 1, GATE_LANES)
    info_row = lax.broadcasted_iota(jnp.int32, (8, rows), 0)
    route_ref[...] = jnp.where(info_row == 0, bucket, jnp.where(info_row == 1, rank, 0.0))
    cnt_ref[...] = run_ref[...]


def _time_index(i, rows, batch, steps, start):
    r = lax.broadcasted_iota(jnp.int32, (rows, 1), 0)
    return start + i * steps + r // batch


def _pool_kernel(x_ref, sh1_ref, sc1_ref, g1_ref, sh2_ref, sc2_ref, n1_ref, n2_ref, buf_ref,
                 wp_ref, ps_ref, rw_ref, rb_ref,
                 x1_ref, h2_ref, route_ref, cnt_ref, nbuf_ref,
                 hb_ref, run_ref, *, batch, steps, n_tiles, start):
    i = pl.program_id(0)
    rows = steps * batch
    halo = POOL_BUF * batch

    @pl.when(i == 0)
    def _():
        hb_ref[0:halo, :] = buf_ref[...]
        run_ref[...] = jnp.zeros_like(run_ref)

    x = x_ref[...]
    h = _norm_mod(x, n1_ref[...], sh1_ref[...], sc1_ref[...])
    hb_ref[halo:halo + rows, :] = h
    pos = _time_index(i, rows, batch, steps, start)
    mixed = []
    for g, w in enumerate(POOL_WINDOWS):
        c0, c1 = g * POOL_GW, (g + 1) * POOL_GW
        acc = hb_ref[halo:halo + rows, c0:c1]
        for k in range(1, w):
            off = (POOL_BUF - k) * batch
            acc = acc + hb_ref[off:off + rows, c0:c1]
        inv_cnt = 1.0 / jnp.minimum(pos + 1, w).astype(jnp.float32)
        pooled = acc * inv_cnt - h[:, c0:c1]
        p_hi, p_lo = _split(pooled)
        mixed.append(_dot3(p_hi, p_lo, wp_ref[0, g], wp_ref[1, g]))
    mix = jnp.concatenate(mixed, axis=-1) * ps_ref[...]
    x1 = _gated_add(x, g1_ref[...], mix)
    x1_ref[...] = x1

    if n_tiles > 1:
        @pl.when(i < n_tiles - 1)
        def _():
            for k in range(POOL_BUF):
                hb_ref[k * batch:(k + 1) * batch, :] = hb_ref[rows + k * batch:rows + (k + 1) * batch, :]

    @pl.when(i == n_tiles - 1)
    def _():
        nbuf_ref[...] = hb_ref[rows:rows + halo, :]

    h2 = _norm_mod(x1, n2_ref[...], sh2_ref[...], sc2_ref[...])
    bucket, rank, wa, wb = _route(h2, rw_ref, rb_ref, run_ref)
    _emit_routed(h2, bucket, rank, wa, wb, h2_ref, route_ref, cnt_ref, run_ref)


def _mod_spec(layer, k, group):
    return pl.BlockSpec((None, None, MOD_ROWS, D), lambda i: (layer, k, group, 0))


def _norm_spec(layer, k):
    return pl.BlockSpec((None, None, 1, D), lambda i: (layer, k, 0, 0))


def _full_spec(shape):
    nd = len(shape)
    return pl.BlockSpec(shape, lambda i: (0,) * nd)


def _const_spec(shape):
    nd = len(shape)
    return pl.BlockSpec(shape, lambda i: (0,) * nd, pipeline_mode=pl.Buffered(1))


def _mixer_out(rows_total, n_tiles, rows):
    out_shape = [
        jax.ShapeDtypeStruct((rows_total, D), jnp.float32),
        jax.ShapeDtypeStruct((rows_total, 1, DX), jnp.float32),
        jax.ShapeDtypeStruct((n_tiles, 8, rows), jnp.float32),
        jax.ShapeDtypeStruct((BUCKET_ROWS, 128), jnp.float32),
    ]
    out_specs = [
        pl.BlockSpec((rows, D), lambda i: (i, 0)),
        pl.BlockSpec((rows, 1, DX), lambda i: (i, 0, 0)),
        pl.BlockSpec((None, 8, rows), lambda i: (i, 0, 0)),
        _full_spec((BUCKET_ROWS, 128)),
    ]
    return out_shape, out_specs


def _pool_layer(x, mod, norm_g4, buf, wp, ps, rw_t, rb, *, layer, group, batch, start):
    rows_total = x.shape[0]
    rows = min(TM_MIX, rows_total)
    steps = rows // batch
    n_tiles = rows_total // rows
    halo = POOL_BUF * batch
    out_shape, out_specs = _mixer_out(rows_total, n_tiles, rows)
    out_shape.append(jax.ShapeDtypeStruct((halo, D), jnp.float32))
    out_specs.append(_full_spec((halo, D)))
    kern = functools.partial(_pool_kernel, batch=batch, steps=steps, n_tiles=n_tiles, start=start)
    return pl.pallas_call(
        kern,
        out_shape=out_shape,
        grid=(n_tiles,),
        in_specs=[
            pl.BlockSpec((rows, D), lambda i: (i, 0)),
            _mod_spec(layer, 0, group), _mod_spec(layer, 1, group), _mod_spec(layer, 2, group),
            _mod_spec(layer, 3, group), _mod_spec(layer, 4, group),
            _norm_spec(layer, 0), _norm_spec(layer, 1),
            _full_spec((halo, D)),
            _full_spec((2, LRU_BLOCKS, POOL_GW, POOL_GW)),
            _full_spec((1, D)),
            _full_spec((N_EXPERTS, D)),
            _full_spec((N_EXPERTS, 1)),
        ],
        out_specs=out_specs,
        scratch_shapes=[
            pltpu.VMEM((halo + rows, D), jnp.float32),
            pltpu.VMEM((BUCKET_ROWS, 128), jnp.float32),
        ],
        compiler_params=pltpu.CompilerParams(
            dimension_semantics=("arbitrary",), vmem_limit_bytes=VMEM_LIMIT),
        name=f"pool_layer_b{batch}",
    )(x, mod, mod, mod, mod, mod, norm_g4, norm_g4, buf, wp, ps, rw_t, rb)


def _lru_kernel(x_ref, sh1_ref, sc1_ref, g1_ref, sh2_ref, sc2_ref, n1_ref, n2_ref,
                cbuf_ref, h0_ref, wx_ref, bx_ref, wy_ref, by_ref, cw_ref, cb_ref,
                wa_ref, ba_ref, wi_ref, bi_ref, lam_ref, wo_ref, bo_ref, rw_ref, rb_ref,
                x1_ref, h2_ref, route_ref, cnt_ref, nconv_ref, nh_ref,
                xp_ref, a_ref, b_ref, hs_ref, hst_ref, run_ref, *, batch, steps, n_tiles, start):
    i = pl.program_id(0)
    rows = steps * batch
    halo = (CONV_WIDTH - 1) * batch

    @pl.when(i == 0)
    def _():
        xp_ref[0:halo, :] = cbuf_ref[...]
        hst_ref[...] = h0_ref[...]
        run_ref[...] = jnp.zeros_like(run_ref)

    x = x_ref[...]
    h_hi, h_lo = _split(_norm_mod(x, n1_ref[...], sh1_ref[...], sc1_ref[...]))
    xb = _dot3(h_hi, h_lo, wx_ref[0], wx_ref[1]) + bx_ref[...]
    yb = jax.nn.gelu(_dot3(h_hi, h_lo, wy_ref[0], wy_ref[1]) + by_ref[...], approximate=True)
    xp_ref[halo:halo + rows, :] = xb
    xc = xp_ref[0:rows, :] * cw_ref[0:1, :]
    for k in range(1, CONV_WIDTH):
        xc = xc + xp_ref[k * batch:k * batch + rows, :] * cw_ref[k:k + 1, :]
    xc = xc + cb_ref[...]
    xc_hi, xc_lo = _split(xc)
    ra, ri = [], []
    for n in range(LRU_BLOCKS):
        cols = slice(n * LRU_BW, (n + 1) * LRU_BW)
        ra.append(_dot3(xc_hi[:, cols], xc_lo[:, cols], wa_ref[0, n], wa_ref[1, n]))
        ri.append(_dot3(xc_hi[:, cols], xc_lo[:, cols], wi_ref[0, n], wi_ref[1, n]))
    r = jax.nn.sigmoid(jnp.concatenate(ra, axis=-1) + ba_ref[...])
    gi = jax.nn.sigmoid(jnp.concatenate(ri, axis=-1) + bi_ref[...])
    neg_lam = -lam_ref[...]
    softplus = jnp.maximum(neg_lam, 0.0) + jnp.log1p(jnp.exp(-jnp.abs(neg_lam)))
    log_a = -LRU_C * r * softplus
    a = jnp.exp(log_a)
    mult = jnp.sqrt(jnp.maximum(1.0 - jnp.exp(2.0 * log_a), 0.0))
    pos = _time_index(i, rows, batch, steps, start)
    mult = jnp.where(pos == 0, 1.0, mult)
    a_ref[...] = a
    b_ref[...] = xc * gi * mult

    hcur = hst_ref[...]
    for t in range(steps):
        sl = slice(t * batch, (t + 1) * batch)
        hcur = a_ref[sl, :] * hcur + b_ref[sl, :]
        hs_ref[sl, :] = hcur
    hst_ref[...] = hcur

    gated_hi, gated_lo = _split(hs_ref[...] * yb)
    y = _dot3(gated_hi, gated_lo, wo_ref[0], wo_ref[1]) + bo_ref[...]
    x1 = _gated_add(x, g1_ref[...], y)
    x1_ref[...] = x1

    if n_tiles > 1:
        @pl.when(i < n_tiles - 1)
        def _():
            for k in range(CONV_WIDTH - 1):
                xp_ref[k * batch:(k + 1) * batch, :] = xp_ref[rows + k * batch:rows + (k + 1) * batch, :]

    @pl.when(i == n_tiles - 1)
    def _():
        nconv_ref[...] = xp_ref[rows:rows + halo, :]
        nh_ref[...] = hcur

    h2 = _norm_mod(x1, n2_ref[...], sh2_ref[...], sc2_ref[...])
    bucket, rank, wa, wb = _route(h2, rw_ref, rb_ref, run_ref)
    _emit_routed(h2, bucket, rank, wa, wb, h2_ref, route_ref, cnt_ref, run_ref)


def _lru_layer(x, mod, norm_g4, cbuf, h0, p, rw_t, rb, *, layer, group, batch, start):
    rows_total = x.shape[0]
    rows = min(TM_MIX, rows_total)
    steps = rows // batch
    n_tiles = rows_total // rows
    halo = (CONV_WIDTH - 1) * batch
    out_shape, out_specs = _mixer_out(rows_total, n_tiles, rows)
    out_shape += [jax.ShapeDtypeStruct((halo, D), jnp.float32),
                  jax.ShapeDtypeStruct((batch, D), jnp.float32)]
    out_specs += [_full_spec((halo, D)), _full_spec((batch, D))]
    kern = functools.partial(_lru_kernel, batch=batch, steps=steps, n_tiles=n_tiles, start=start)
    vec = _full_spec((1, D))
    sq = _const_spec((2, D, D))
    blk = _const_spec((2, LRU_BLOCKS, LRU_BW, LRU_BW))
    return pl.pallas_call(
        kern,
        out_shape=out_shape,
        grid=(n_tiles,),
        in_specs=[
            pl.BlockSpec((rows, D), lambda i: (i, 0)),
            _mod_spec(layer, 0, group), _mod_spec(layer, 1, group), _mod_spec(layer, 2, group),
            _mod_spec(layer, 3, group), _mod_spec(layer, 4, group),
            _norm_spec(layer, 0), _norm_spec(layer, 1),
            _full_spec((halo, D)), _full_spec((batch, D)),
            sq, vec, sq, vec, _full_spec((CONV_WIDTH, D)), vec,
            blk, vec, blk, vec, vec, sq, vec,
            _full_spec((N_EXPERTS, D)), _full_spec((N_EXPERTS, 1)),
        ],
        out_specs=out_specs,
        scratch_shapes=[
            pltpu.VMEM((halo + rows, D), jnp.float32),
            pltpu.VMEM((rows, D), jnp.float32),
            pltpu.VMEM((rows, D), jnp.float32),
            pltpu.VMEM((rows, D), jnp.float32),
            pltpu.VMEM((batch, D), jnp.float32),
            pltpu.VMEM((BUCKET_ROWS, 128), jnp.float32),
        ],
        compiler_params=pltpu.CompilerParams(
            dimension_semantics=("arbitrary",), vmem_limit_bytes=VMEM_LIMIT),
        name=f"lru_layer_b{batch}",
    )(x, mod, mod, mod, mod, mod, norm_g4, norm_g4, cbuf, h0,
      p["w_x"], p["b_x"], p["w_y"], p["b_y"], p["cw"], p["cb"],
      p["w_a"], p["b_a"], p["w_i"], p["b_i"], p["lam"], p["w_out"], p["b_out"], rw_t, rb)


def _row_copy_wait(src_ref, dst_ref, sem):
    pltpu.make_async_copy(src_ref, dst_ref, sem).wait()


def _dispatch_kernel(pos_ref, hp_ref, hs_ref, zero_ref, xs_ref, sem, *, n_prompt_tiles):
    del zero_ref
    i = pl.program_id(0)

    def scatter(src_ref):
        def body(r, carry):
            p = pos_ref[i * TM + r]
            pltpu.make_async_copy(src_ref.at[r], xs_ref.at[p], sem).start()
            return carry
        lax.fori_loop(0, TM, body, 0)
        _row_copy_wait(src_ref, xs_ref.at[pl.ds(0, TM)], sem)

    @pl.when(i < n_prompt_tiles)
    def _():
        scatter(hp_ref)

    @pl.when(i >= n_prompt_tiles)
    def _():
        scatter(hs_ref)


def _dispatch(pos, h2_p, h2_s, n_sorted):
    n_p = h2_p.shape[0] // TM
    n_s = h2_s.shape[0] // TM
    zeros = jnp.zeros((n_sorted, 1, DX), jnp.float32)
    grid_spec = pltpu.PrefetchScalarGridSpec(
        num_scalar_prefetch=1,
        grid=(n_p + n_s,),
        in_specs=[
            pl.BlockSpec((TM, 1, DX), lambda i, pos: (jnp.minimum(i, n_p - 1), 0, 0)),
            pl.BlockSpec((TM, 1, DX), lambda i, pos: (jnp.maximum(i - n_p, 0), 0, 0)),
            pl.BlockSpec(memory_space=pl.ANY),
        ],
        out_specs=pl.BlockSpec(memory_space=pl.ANY),
        scratch_shapes=[pltpu.SemaphoreType.DMA],
    )
    return pl.pallas_call(
        functools.partial(_dispatch_kernel, n_prompt_tiles=n_p),
        out_shape=jax.ShapeDtypeStruct((n_sorted, 1, DX), jnp.float32),
        grid_spec=grid_spec,
        input_output_aliases={3: 0},
        compiler_params=pltpu.CompilerParams(
            dimension_semantics=("arbitrary",), vmem_limit_bytes=VMEM_LIMIT),
        name="dispatch",
    )(pos, h2_p, h2_s, zeros)


def _expert_kernel(ea_ref, eb_ref, nu_ref, xs_ref, wga_ref, wua_ref, wda_ref, wgb_ref, wub_ref,
                   wdb_ref, ys_ref, x2d_ref):
    j = pl.program_id(0)

    @pl.when(j < nu_ref[0])
    def _():
        x2d_ref[...] = xs_ref[...].reshape(TMM, DX)
        x_hi, x_lo = _split(x2d_ref[:, 0:D])
        gate_a = x2d_ref[:, D:D + 1]
        gate_b = x2d_ref[:, D + 1:D + 2]

        def expert(wg_ref, wu_ref, wd_ref, gate):
            hg = _dot3(x_hi, x_lo, wg_ref[0], wg_ref[1])
            hu = _dot3(x_hi, x_lo, wu_ref[0], wu_ref[1])
            hid_hi, hid_lo = _split(hg * jax.nn.sigmoid(hg) * hu * gate)
            return _dot3(hid_hi, hid_lo, wd_ref[0], wd_ref[1])

        y = expert(wga_ref, wua_ref, wda_ref, gate_a) + expert(wgb_ref, wub_ref, wdb_ref, gate_b)
        ys_ref[...] = y.reshape(TMM, 1, D)

    @pl.when(j >= nu_ref[0])
    def _():
        ys_ref[...] = jnp.zeros_like(ys_ref)


def _experts(ea, eb, n_used, xs, wg, wu, wd):
    n_tiles = xs.shape[0] // TMM

    def x_map(j, ea, eb, nu):
        return (jnp.minimum(j, nu[0] - 1), 0, 0)

    def a_map(j, ea, eb, nu):
        return (0, ea[j], 0, 0)

    def b_map(j, ea, eb, nu):
        return (0, eb[j], 0, 0)

    up = (2, None, D, D_FF)
    down = (2, None, D_FF, D)
    grid_spec = pltpu.PrefetchScalarGridSpec(
        num_scalar_prefetch=3,
        grid=(n_tiles,),
        in_specs=[
            pl.BlockSpec((TMM, 1, DX), x_map),
            pl.BlockSpec(up, a_map), pl.BlockSpec(up, a_map), pl.BlockSpec(down, a_map),
            pl.BlockSpec(up, b_map), pl.BlockSpec(up, b_map), pl.BlockSpec(down, b_map),
        ],
        out_specs=pl.BlockSpec((TMM, 1, D), lambda j, ea, eb, nu: (j, 0, 0)),
        scratch_shapes=[pltpu.VMEM((TMM, DX), jnp.float32)],
    )
    return pl.pallas_call(
        _expert_kernel,
        out_shape=jax.ShapeDtypeStruct((xs.shape[0], 1, D), jnp.float32),
        grid_spec=grid_spec,
        compiler_params=pltpu.CompilerParams(
            dimension_semantics=("arbitrary",), vmem_limit_bytes=VMEM_LIMIT),
        name="experts",
    )(ea, eb, n_used, xs, wg, wu, wd, wg, wu, wd)


def _combine_kernel(pos_ref, xp_ref, xs_ref, gp_ref, gs_ref, fg_ref, ys_ref, op_ref, os_ref,
                    g3_ref, g2_ref, sem, *, n_prompt_tiles, final):
    i = pl.program_id(0)

    def body(r, carry):
        p = pos_ref[i * TM + r]
        pltpu.make_async_copy(ys_ref.at[p], g3_ref.at[r], sem).start()
        return carry
    lax.fori_loop(0, TM, body, 0)
    _row_copy_wait(ys_ref.at[pl.ds(0, TM)], g3_ref, sem)
    g2_ref[...] = g3_ref[...].reshape(TM, D)

    def finish(x_ref, gate_ref, o_ref):
        x2 = _gated_add(x_ref[...], gate_ref[...], g2_ref[...])
        if final:
            ms = jnp.mean(x2 * x2, axis=-1, keepdims=True)
            x2 = x2 * lax.rsqrt(ms + EPS) * fg_ref[...]
        o_ref[...] = x2

    @pl.when(i < n_prompt_tiles)
    def _():
        finish(xp_ref, gp_ref, op_ref)

    @pl.when(i >= n_prompt_tiles)
    def _():
        finish(xs_ref, gs_ref, os_ref)


def _combine(pos, x1_p, x1_s, mod, final_g, ys, *, layer, final):
    n_p = x1_p.shape[0] // TM
    n_s = x1_s.shape[0] // TM

    def p_map(i, pos):
        return (jnp.minimum(i, n_p - 1), 0)

    def s_map(i, pos):
        return (jnp.maximum(i - n_p, 0), 0)

    grid_spec = pltpu.PrefetchScalarGridSpec(
        num_scalar_prefetch=1,
        grid=(n_p + n_s,),
        in_specs=[
            pl.BlockSpec((TM, D), p_map),
            pl.BlockSpec((TM, D), s_map),
            pl.BlockSpec((None, None, MOD_ROWS, D), lambda i, pos: (layer, 5, 0, 0)),
            pl.BlockSpec((None, None, MOD_ROWS, D), lambda i, pos: (layer, 5, 1, 0)),
            pl.BlockSpec((1, D), lambda i, pos: (0, 0)),
            pl.BlockSpec(memory_space=pl.ANY),
        ],
        out_specs=[pl.BlockSpec((TM, D), p_map), pl.BlockSpec((TM, D), s_map)],
        scratch_shapes=[
            pltpu.VMEM((TM, 1, D), jnp.float32),
            pltpu.VMEM((TM, D), jnp.float32),
            pltpu.SemaphoreType.DMA,
        ],
    )
    return pl.pallas_call(
        functools.partial(_combine_kernel, n_prompt_tiles=n_p, final=final),
        out_shape=[jax.ShapeDtypeStruct(x1_p.shape, jnp.float32),
                   jax.ShapeDtypeStruct(x1_s.shape, jnp.float32)],
        grid_spec=grid_spec,
        compiler_params=pltpu.CompilerParams(
            dimension_semantics=("arbitrary",), vmem_limit_bytes=VMEM_LIMIT),
        name="combine",
    )(pos, x1_p, x1_s, mod, mod, final_g, ys)


def _plan(route_p, route_s, cnt_p, cnt_s, n_sorted_tiles):
    b_p = route_p[:, 0, :].reshape(-1).astype(jnp.int32)
    r_p = route_p[:, 1, :].reshape(-1).astype(jnp.int32)
    b_s = route_s[:, 0, :].reshape(-1).astype(jnp.int32)
    r_s = route_s[:, 1, :].reshape(-1).astype(jnp.int32)
    c_p = cnt_p[:, 0].astype(jnp.int32)
    c_s = cnt_s[:, 0].astype(jnp.int32)
    tiles_per_bucket = (c_p + c_s + TMM - 1) // TMM
    tile_end = jnp.cumsum(tiles_per_bucket)
    row_start = (tile_end - tiles_per_bucket) * TMM
    pos = jnp.concatenate([row_start[b_p] + r_p, row_start[b_s] + c_p[b_s] + r_s])
    n_used = tile_end[-1]
    tile = jnp.minimum(jnp.arange(n_sorted_tiles, dtype=jnp.int32), n_used - 1)
    tile_bucket = jnp.sum((tile[:, None] >= tile_end[None, :]).astype(jnp.int32), axis=1)
    group = tile_bucket // 6
    pair = tile_bucket % 6
    ea = GROUP_SIZE * group + jnp.asarray(_PAIR_A, jnp.int32)[pair]
    eb = GROUP_SIZE * group + jnp.asarray(_PAIR_B, jnp.int32)[pair]
    return pos, ea, eb, n_used.reshape(1)


def _time_major(x):
    b, s, d = x.shape
    return jnp.transpose(x, (1, 0, 2)).reshape(s * b, d)


def _batch_major(x, batch):
    rows, d = x.shape
    return jnp.transpose(x.reshape(rows // batch, batch, d), (1, 0, 2))


def kernel(x_prompt, x_sample, c_prompt, c_sample, state_pool_buf, state_conv_buf, state_lru_h,
           w_mod, b_mod, norm_g, pool_w, pool_scale, lru_w_x, lru_b_x, lru_w_y, lru_b_y,
           conv_w, conv_b, lru_w_a, lru_b_a, lru_w_i, lru_b_i, lru_lambda, lru_w_out, lru_b_out,
           router_w, router_b, moe_w_gate, moe_w_up, moe_w_down, final_g):
    bp, sp, _ = x_prompt.shape
    bs, ss, _ = x_sample.shape
    f32 = jnp.float32

    c_exp = jnp.concatenate([jnp.tile(c_prompt, (MOD_ROWS // bp, 1)),
                             jnp.tile(c_sample, (MOD_ROWS // bs, 1))], axis=0)
    mod = _modulation(c_exp, w_mod, b_mod)
    norm_g4 = norm_g.reshape(DEPTH, 2, 1, D)
    rw_t = router_w.T
    rb = router_b.reshape(N_EXPERTS, 1)
    wg2, wu2, wd2 = (_split_weight(w, axis=1) for w in (moe_w_gate, moe_w_up, moe_w_down))
    fg = final_g.reshape(1, D)

    xp = _time_major(x_prompt)
    xs = _time_major(x_sample)
    n_tokens = xp.shape[0] + xs.shape[0]
    n_sorted_tiles = n_tokens // TMM + N_BUCKETS
    n_sorted = n_sorted_tiles * TMM

    groups = (
        dict(batch=bp, start=0, group=0),
        dict(batch=bs, start=PAST_LEN, group=1),
    )
    new_pool = ([], [])
    new_conv = ([], [])
    new_h = ([], [])
    x = [xp, xs]
    for layer in range(DEPTH):
        j = layer // 2
        outs = []
        for gi, g in enumerate(groups):
            b = g["batch"]
            if layer % 2 == 0:
                if gi == 0:
                    buf = jnp.zeros((POOL_BUF * b, D), f32)
                else:
                    buf = _time_major(state_pool_buf[j])
                x1, h2, route, cnt, nbuf = _pool_layer(
                    x[gi], mod, norm_g4, buf, _split_weight(pool_w[j]), pool_scale[j].reshape(1, D),
                    rw_t, rb, layer=layer, **g)
                new_pool[gi].append(_batch_major(nbuf, b))
            else:
                if gi == 0:
                    cbuf = jnp.zeros(((CONV_WIDTH - 1) * b, D), f32)
                    h0 = jnp.zeros((b, D), f32)
                else:
                    cbuf = _time_major(state_conv_buf[j])
                    h0 = state_lru_h[j]
                p = dict(
                    w_x=_split_weight(lru_w_x[j]), b_x=lru_b_x[j].reshape(1, D),
                    w_y=_split_weight(lru_w_y[j]), b_y=lru_b_y[j].reshape(1, D),
                    cw=conv_w[j], cb=conv_b[j].reshape(1, D),
                    w_a=_split_weight(lru_w_a[j]), b_a=lru_b_a[j].reshape(1, D),
                    w_i=_split_weight(lru_w_i[j]), b_i=lru_b_i[j].reshape(1, D),
                    lam=lru_lambda[j].reshape(1, D),
                    w_out=_split_weight(lru_w_out[j]), b_out=lru_b_out[j].reshape(1, D))
                x1, h2, route, cnt, nconv, nh = _lru_layer(
                    x[gi], mod, norm_g4, cbuf, h0, p, rw_t, rb, layer=layer, **g)
                new_conv[gi].append(_batch_major(nconv, b))
                new_h[gi].append(nh)
            outs.append((x1, h2, route, cnt))
        (x1_p, h2_p, route_p, cnt_p), (x1_s, h2_s, route_s, cnt_s) = outs
        pos, ea, eb, n_used = _plan(route_p, route_s, cnt_p, cnt_s, n_sorted_tiles)
        xsorted = _dispatch(pos, h2_p, h2_s, n_sorted)
        ysorted = _experts(ea, eb, n_used, xsorted, wg2[layer], wu2[layer], wd2[layer])
        x = list(_combine(pos, x1_p, x1_s, mod, fg, ysorted, layer=layer,
                          final=(layer == DEPTH - 1)))

    y_prompt = _batch_major(x[0], bp)
    y_sample = _batch_major(x[1], bs)
    return (y_prompt, y_sample,
            jnp.stack(new_pool[0]), jnp.stack(new_pool[1]),
            jnp.stack(new_conv[0]), jnp.stack(new_conv[1]),
            jnp.stack(new_h[0]), jnp.stack(new_h[1]))
```

```python
import functools

import jax
import jax.numpy as jnp
from jax import lax
from jax.experimental import pallas as pl
from jax.experimental.pallas import tpu as pltpu

D = 1024
DEPTH = 4
PAST_LEN = 16384
POOL_WINDOWS = (2, 4, 8, 16)
POOL_GW = 256
POOL_BUF = 15
LRU_BLOCKS = 4
LRU_BW = 256
CONV_WIDTH = 4
LRU_C = 8.0
N_EXPERTS = 16
GROUP_SIZE = 4
D_FF = 512
EPS = 1e-6

MOD_ROWS = 128
TM = 512
TM_MIX = 256
TMM = 256
SPLIT_ROWS = 128
GATE_LANES = 128
DX = D + GATE_LANES
N_BUCKETS = 24
BUCKET_ROWS = 32
VMEM_LIMIT = 56 * 1024 * 1024

_PAIR_A = (0, 0, 0, 1, 1, 2)
_PAIR_B = (1, 2, 3, 2, 3, 3)


def _bf(x):
    return x.astype(jnp.bfloat16)


def _split(x):
    hi = _bf(x)
    lo = _bf(x - hi.astype(jnp.float32))
    return hi, lo


def _split_weight(w):
    bits = lax.bitcast_convert_type(w, jnp.uint32) & jnp.uint32(0xFFFF0000)
    hi = lax.bitcast_convert_type(bits, jnp.float32)
    return jnp.stack([_bf(hi), _bf(w - hi)], axis=0)


def _dot(a, b):
    return jnp.dot(a, b, preferred_element_type=jnp.float32)


def _dot3(a_hi, a_lo, w_hi, w_lo):
    rows = a_hi.shape[0]
    both = _dot(jnp.concatenate([a_hi, a_lo], axis=0), w_hi)
    return both[0:rows] + both[rows:2 * rows] + _dot(a_hi, w_lo)


def _dot_nt(a, b):
    return lax.dot_general(a, b, (((1,), (1,)), ((), ())), preferred_element_type=jnp.float32)


def _mod_kernel(c_ref, w_ref, b_ref, o_ref):
    c = c_ref[...]
    cs = c * jax.nn.sigmoid(c)
    c_hi, c_lo = _split(cs)
    w_hi, w_lo = _split(w_ref[...])
    o_ref[...] = _dot(c_hi, w_hi) + _dot(c_lo, w_hi) + _dot(c_hi, w_lo) + b_ref[...]


def _modulation(c_exp, w_mod, b_mod):
    rows = c_exp.shape[0]
    return pl.pallas_call(
        _mod_kernel,
        out_shape=jax.ShapeDtypeStruct((DEPTH, 6, rows, D), jnp.float32),
        grid=(DEPTH, 6),
        in_specs=[
            pl.BlockSpec((rows, D), lambda i, n: (0, 0)),
            pl.BlockSpec((None, D, D), lambda i, n: (i, 0, n)),
            pl.BlockSpec((None, None, 1, D), lambda i, n: (i, n, 0, 0)),
        ],
        out_specs=pl.BlockSpec((None, None, rows, D), lambda i, n: (i, n, 0, 0)),
        compiler_params=pltpu.CompilerParams(
            dimension_semantics=("arbitrary", "arbitrary"), vmem_limit_bytes=VMEM_LIMIT),
        name="modulation",
    )(c_exp, w_mod, b_mod.reshape(DEPTH, 6, 1, D))


def _norm_mod(x, gamma, shift, scale):
    rows = x.shape[0]
    ms = jnp.mean(x * x, axis=-1, keepdims=True)
    y = x * lax.rsqrt(ms + EPS) * gamma
    y3 = y.reshape(rows // MOD_ROWS, MOD_ROWS, D)
    return (y3 * (1.0 + scale)[None] + shift[None]).reshape(rows, D)


def _gated_add(x, gate, y):
    rows = x.shape[0]
    y3 = y.reshape(rows // MOD_ROWS, MOD_ROWS, D)
    return x + (y3 * gate[None]).reshape(rows, D)


def _route(h2, rw_ref, rb_ref, run_ref):
    rows = h2.shape[0]
    h_hi, h_lo = _split(h2)
    w_hi, w_lo = _split(rw_ref[...])
    logits = _dot_nt(w_hi, h_hi) + _dot_nt(w_lo, h_hi) + _dot_nt(w_hi, h_lo)
    m = jnp.max(logits, axis=0, keepdims=True)
    e = jnp.exp(logits - m)
    scores = e / jnp.sum(e, axis=0, keepdims=True)
    sel = scores + rb_ref[...]
    eidx = lax.broadcasted_iota(jnp.int32, (N_EXPERTS, rows), 0).astype(jnp.float32)
    gidx = jnp.floor(eidx * (1.0 / GROUP_SIZE))
    neg = -jnp.inf
    sentinel = float(N_EXPERTS)

    def top2(vals):
        v1 = jnp.max(vals, axis=0, keepdims=True)
        i1 = jnp.min(jnp.where(vals == v1, eidx, sentinel), axis=0, keepdims=True)
        rest = jnp.where(eidx == i1, neg, vals)
        v2 = jnp.max(rest, axis=0, keepdims=True)
        i2 = jnp.min(jnp.where(rest == v2, eidx, sentinel), axis=0, keepdims=True)
        return v1, i1, v2, i2

    best = jnp.zeros((1, rows), jnp.float32)
    best_v = None
    for g in range(N_EXPERTS // GROUP_SIZE):
        v1, _, v2, _ = top2(jnp.where(gidx == float(g), sel, neg))
        gs = v1 + v2
        if best_v is None:
            best_v = gs
        else:
            upd = gs > best_v
            best = jnp.where(upd, float(g), best)
            best_v = jnp.where(upd, gs, best_v)
    _, i1, _, i2 = top2(jnp.where(gidx == best, sel, neg))
    chosen = (eidx == i1) | (eidx == i2)
    ssum = jnp.sum(jnp.where(chosen, scores, 0.0), axis=0, keepdims=True)
    gates = jnp.where(chosen, scores / ssum, 0.0)
    ea = jnp.minimum(i1, i2)
    eb = jnp.maximum(i1, i2)
    wa = jnp.sum(jnp.where(eidx == ea, gates, 0.0), axis=0, keepdims=True)
    wb = jnp.sum(jnp.where(eidx == eb, gates, 0.0), axis=0, keepdims=True)
    a = ea - GROUP_SIZE * best
    b = eb - GROUP_SIZE * best
    pair = jnp.where(a == 0.0, b - 1.0, jnp.where(a == 1.0, b + 1.0, 5.0))
    bucket = 6.0 * best + pair

    bidx = lax.broadcasted_iota(jnp.int32, (BUCKET_ROWS, rows), 0).astype(jnp.float32)
    onehot = jnp.where(bidx == bucket, 1.0, 0.0)
    src = lax.broadcasted_iota(jnp.int32, (rows, rows), 0)
    dst = lax.broadcasted_iota(jnp.int32, (rows, rows), 1)
    before = _bf(jnp.where(src < dst, 1.0, 0.0))
    earlier = _dot(_bf(onehot), before)
    base = run_ref[:, 0:1]
    rank = jnp.sum(onehot * (earlier + base), axis=0, keepdims=True)
    run_ref[...] = run_ref[...] + jnp.sum(onehot, axis=1, keepdims=True)
    return bucket, rank, wa, wb


def _emit_routed(h2, bucket, rank, wa, wb, h2_ref, route_ref, cnt_ref, run_ref):
    rows = h2.shape[0]
    lane_row = lax.broadcasted_iota(jnp.int32, (GATE_LANES, rows), 0)
    gate_t = jnp.where(lane_row == 0, wa, jnp.where(lane_row == 1, wb, 0.0))
    h2_ref[:, :, 0:D] = h2.reshape(rows, 1, D)
    h2_ref[:, :, D:DX] = gate_t.T.reshape(rows, 1, GATE_LANES)
    info_row = lax.broadcasted_iota(jnp.int32, (8, rows), 0)
    route_ref[...] = jnp.where(info_row == 0, bucket, jnp.where(info_row == 1, rank, 0.0))
    cnt_ref[...] = run_ref[...]


def _time_index(i, rows, batch, steps, start):
    r = lax.broadcasted_iota(jnp.int32, (rows, 1), 0)
    return start + i * steps + r // batch


def _pool_kernel(x_ref, sh1_ref, sc1_ref, g1_ref, sh2_ref, sc2_ref, n1_ref, n2_ref, buf_ref,
                 wp_ref, ps_ref, rw_ref, rb_ref,
                 x1_ref, h2_ref, route_ref, cnt_ref, nbuf_ref,
                 hb_ref, run_ref, *, batch, steps, n_tiles, start):
    i = pl.program_id(0)
    rows = steps * batch
    halo = POOL_BUF * batch

    @pl.when(i == 0)
    def _():
        hb_ref[0:halo, :] = buf_ref[...]
        run_ref[...] = jnp.zeros_like(run_ref)

    x = x_ref[...]
    h = _norm_mod(x, n1_ref[...], sh1_ref[...], sc1_ref[...])
    hb_ref[halo:halo + rows, :] = h
    pos = _time_index(i, rows, batch, steps, start)
    mixed = []
    for g, w in enumerate(POOL_WINDOWS):
        c0, c1 = g * POOL_GW, (g + 1) * POOL_GW
        acc = hb_ref[halo:halo + rows, c0:c1]
        for k in range(1, w):
            off = (POOL_BUF - k) * batch
            acc = acc + hb_ref[off:off + rows, c0:c1]
        inv_cnt = 1.0 / jnp.minimum(pos + 1, w).astype(jnp.float32)
        pooled = acc * inv_cnt - h[:, c0:c1]
        p_hi, p_lo = _split(pooled)
        mixed.append(_dot3(p_hi, p_lo, wp_ref[0, g], wp_ref[1, g]))
    mix = jnp.concatenate(mixed, axis=-1) * ps_ref[...]
    x1 = _gated_add(x, g1_ref[...], mix)
    x1_ref[...] = x1

    if n_tiles > 1:
        @pl.when(i < n_tiles - 1)
        def _():
            for k in range(POOL_BUF):
                hb_ref[k * batch:(k + 1) * batch, :] = hb_ref[rows + k * batch:rows + (k + 1) * batch, :]

    @pl.when(i == n_tiles - 1)
    def _():
        nbuf_ref[...] = hb_ref[rows:rows + halo, :]

    h2 = _norm_mod(x1, n2_ref[...], sh2_ref[...], sc2_ref[...])
    bucket, rank, wa, wb = _route(h2, rw_ref, rb_ref, run_ref)
    _emit_routed(h2, bucket, rank, wa, wb, h2_ref, route_ref, cnt_ref, run_ref)


def _mod_spec(layer, k, group):
    return pl.BlockSpec((None, None, MOD_ROWS, D), lambda i: (layer, k, group, 0))


def _norm_spec(layer, k):
    return pl.BlockSpec((None, None, 1, D), lambda i: (layer, k, 0, 0))


def _full_spec(shape):
    nd = len(shape)
    return pl.BlockSpec(shape, lambda i: (0,) * nd)


def _const_spec(shape):
    nd = len(shape)
    return pl.BlockSpec(shape, lambda i: (0,) * nd, pipeline_mode=pl.Buffered(1))


def _mixer_out(rows_total, n_tiles, rows):
    out_shape = [
        jax.ShapeDtypeStruct((rows_total, D), jnp.float32),
        jax.ShapeDtypeStruct((rows_total, 1, DX), jnp.float32),
        jax.ShapeDtypeStruct((n_tiles, 8, rows), jnp.float32),
        jax.ShapeDtypeStruct((BUCKET_ROWS, 128), jnp.float32),
    ]
    out_specs = [
        pl.BlockSpec((rows, D), lambda i: (i, 0)),
        pl.BlockSpec((rows, 1, DX), lambda i: (i, 0, 0)),
        pl.BlockSpec((None, 8, rows), lambda i: (i, 0, 0)),
        _full_spec((BUCKET_ROWS, 128)),
    ]
    return out_shape, out_specs


def _pool_layer(x, mod, norm_g4, buf, wp, ps, rw_t, rb, *, layer, group, batch, start):
    rows_total = x.shape[0]
    rows = min(TM_MIX, rows_total)
    steps = rows // batch
    n_tiles = rows_total // rows
    halo = POOL_BUF * batch
    out_shape, out_specs = _mixer_out(rows_total, n_tiles, rows)
    out_shape.append(jax.ShapeDtypeStruct((halo, D), jnp.float32))
    out_specs.append(_full_spec((halo, D)))
    kern = functools.partial(_pool_kernel, batch=batch, steps=steps, n_tiles=n_tiles, start=start)
    return pl.pallas_call(
        kern,
        out_shape=out_shape,
        grid=(n_tiles,),
        in_specs=[
            pl.BlockSpec((rows, D), lambda i: (i, 0)),
            _mod_spec(layer, 0, group), _mod_spec(layer, 1, group), _mod_spec(layer, 2, group),
            _mod_spec(layer, 3, group), _mod_spec(layer, 4, group),
            _norm_spec(layer, 0), _norm_spec(layer, 1),
            _full_spec((halo, D)),
            _full_spec((2, LRU_BLOCKS, POOL_GW, POOL_GW)),
            _full_spec((1, D)),
            _full_spec((N_EXPERTS, D)),
            _full_spec((N_EXPERTS, 1)),
        ],
        out_specs=out_specs,
        scratch_shapes=[
            pltpu.VMEM((halo + rows, D), jnp.float32),
            pltpu.VMEM((BUCKET_ROWS, 128), jnp.float32),
        ],
        compiler_params=pltpu.CompilerParams(
            dimension_semantics=("arbitrary",), vmem_limit_bytes=VMEM_LIMIT),
        name=f"pool_layer_b{batch}",
    )(x, mod, mod, mod, mod, mod, norm_g4, norm_g4, buf, wp, ps, rw_t, rb)


def _lru_kernel(x_ref, sh1_ref, sc1_ref, g1_ref, sh2_ref, sc2_ref, n1_ref, n2_ref,
                cbuf_ref, h0_ref, wx_ref, bx_ref, wy_ref, by_ref, cw_ref, cb_ref,
                wa_ref, ba_ref, wi_ref, bi_ref, lam_ref, wo_ref, bo_ref, rw_ref, rb_ref,
                x1_ref, h2_ref, route_ref, cnt_ref, nconv_ref, nh_ref,
                xp_ref, a_ref, b_ref, hs_ref, hst_ref, run_ref, *, batch, steps, n_tiles, start):
    i = pl.program_id(0)
    rows = steps * batch
    halo = (CONV_WIDTH - 1) * batch

    @pl.when(i == 0)
    def _():
        xp_ref[0:halo, :] = cbuf_ref[...]
        hst_ref[...] = h0_ref[...]
        run_ref[...] = jnp.zeros_like(run_ref)

    x = x_ref[...]
    h_hi, h_lo = _split(_norm_mod(x, n1_ref[...], sh1_ref[...], sc1_ref[...]))
    xb = _dot3(h_hi, h_lo, wx_ref[0], wx_ref[1]) + bx_ref[...]
    yb = jax.nn.gelu(_dot3(h_hi, h_lo, wy_ref[0], wy_ref[1]) + by_ref[...], approximate=True)
    xp_ref[halo:halo + rows, :] = xb
    xc = xp_ref[0:rows, :] * cw_ref[0:1, :]
    for k in range(1, CONV_WIDTH):
        xc = xc + xp_ref[k * batch:k * batch + rows, :] * cw_ref[k:k + 1, :]
    xc = xc + cb_ref[...]
    xc_hi, xc_lo = _split(xc)
    ra, ri = [], []
    for n in range(LRU_BLOCKS):
        cols = slice(n * LRU_BW, (n + 1) * LRU_BW)
        ra.append(_dot3(xc_hi[:, cols], xc_lo[:, cols], wa_ref[0, n], wa_ref[1, n]))
        ri.append(_dot3(xc_hi[:, cols], xc_lo[:, cols], wi_ref[0, n], wi_ref[1, n]))
    r = jax.nn.sigmoid(jnp.concatenate(ra, axis=-1) + ba_ref[...])
    gi = jax.nn.sigmoid(jnp.concatenate(ri, axis=-1) + bi_ref[...])
    neg_lam = -lam_ref[...]
    softplus = jnp.maximum(neg_lam, 0.0) + jnp.log1p(jnp.exp(-jnp.abs(neg_lam)))
    log_a = -LRU_C * r * softplus
    a = jnp.exp(log_a)
    mult = jnp.sqrt(jnp.maximum(1.0 - jnp.exp(2.0 * log_a), 0.0))
    pos = _time_index(i, rows, batch, steps, start)
    mult = jnp.where(pos == 0, 1.0, mult)
    a_ref[...] = a
    b_ref[...] = xc * gi * mult

    hcur = hst_ref[...]
    for t in range(steps):
        sl = slice(t * batch, (t + 1) * batch)
        hcur = a_ref[sl, :] * hcur + b_ref[sl, :]
        hs_ref[sl, :] = hcur
    hst_ref[...] = hcur

    gated_hi, gated_lo = _split(hs_ref[...] * yb)
    y = _dot3(gated_hi, gated_lo, wo_ref[0], wo_ref[1]) + bo_ref[...]
    x1 = _gated_add(x, g1_ref[...], y)
    x1_ref[...] = x1

    if n_tiles > 1:
        @pl.when(i < n_tiles - 1)
        def _():
            for k in range(CONV_WIDTH - 1):
                xp_ref[k * batch:(k + 1) * batch, :] = xp_ref[rows + k * batch:rows + (k + 1) * batch, :]

    @pl.when(i == n_tiles - 1)
    def _():
        nconv_ref[...] = xp_ref[rows:rows + halo, :]
        nh_ref[...] = hcur

    h2 = _norm_mod(x1, n2_ref[...], sh2_ref[...], sc2_ref[...])
    bucket, rank, wa, wb = _route(h2, rw_ref, rb_ref, run_ref)
    _emit_routed(h2, bucket, rank, wa, wb, h2_ref, route_ref, cnt_ref, run_ref)


def _lru_layer(x, mod, norm_g4, cbuf, h0, p, rw_t, rb, *, layer, group, batch, start):
    rows_total = x.shape[0]
    rows = min(TM_MIX, rows_total)
    steps = rows // batch
    n_tiles = rows_total // rows
    halo = (CONV_WIDTH - 1) * batch
    out_shape, out_specs = _mixer_out(rows_total, n_tiles, rows)
    out_shape += [jax.ShapeDtypeStruct((halo, D), jnp.float32),
                  jax.ShapeDtypeStruct((batch, D), jnp.float32)]
    out_specs += [_full_spec((halo, D)), _full_spec((batch, D))]
    kern = functools.partial(_lru_kernel, batch=batch, steps=steps, n_tiles=n_tiles, start=start)
    vec = _full_spec((1, D))
    sq = _const_spec((2, D, D))
    blk = _const_spec((2, LRU_BLOCKS, LRU_BW, LRU_BW))
    return pl.pallas_call(
        kern,
        out_shape=out_shape,
        grid=(n_tiles,),
        in_specs=[
            pl.BlockSpec((rows, D), lambda i: (i, 0)),
            _mod_spec(layer, 0, group), _mod_spec(layer, 1, group), _mod_spec(layer, 2, group),
            _mod_spec(layer, 3, group), _mod_spec(layer, 4, group),
            _norm_spec(layer, 0), _norm_spec(layer, 1),
            _full_spec((halo, D)), _full_spec((batch, D)),
            sq, vec, sq, vec, _full_spec((CONV_WIDTH, D)), vec,
            blk, vec, blk, vec, vec, sq, vec,
            _full_spec((N_EXPERTS, D)), _full_spec((N_EXPERTS, 1)),
        ],
        out_specs=out_specs,
        scratch_shapes=[
            pltpu.VMEM((halo + rows, D), jnp.float32),
            pltpu.VMEM((rows, D), jnp.float32),
            pltpu.VMEM((rows, D), jnp.float32),
            pltpu.VMEM((rows, D), jnp.float32),
            pltpu.VMEM((batch, D), jnp.float32),
            pltpu.VMEM((BUCKET_ROWS, 128), jnp.float32),
        ],
        compiler_params=pltpu.CompilerParams(
            dimension_semantics=("arbitrary",), vmem_limit_bytes=VMEM_LIMIT),
        name=f"lru_layer_b{batch}",
    )(x, mod, mod, mod, mod, mod, norm_g4, norm_g4, cbuf, h0,
      p["w_x"], p["b_x"], p["w_y"], p["b_y"], p["cw"], p["cb"],
      p["w_a"], p["b_a"], p["w_i"], p["b_i"], p["lam"], p["w_out"], p["b_out"], rw_t, rb)


def _row_copy_wait(src_ref, dst_ref, sem):
    pltpu.make_async_copy(src_ref, dst_ref, sem).wait()


def _dispatch_kernel(pos_ref, zt_ref, hp_ref, hs_ref, xs_ref, zero_ref, sem, zsem, *,
                     n_prompt_tiles, n_sorted_tiles):
    i = pl.program_id(0)

    @pl.when(i == 0)
    def _():
        zero_ref[...] = jnp.zeros_like(zero_ref)

        def fill(t):
            return pltpu.make_async_copy(zero_ref, xs_ref.at[pl.ds(t * TMM, TMM)], zsem)

        def start(t, carry):
            @pl.when(zt_ref[t] != 0)
            def _():
                fill(t).start()
            return carry

        def wait(t, carry):
            @pl.when(zt_ref[t] != 0)
            def _():
                fill(t).wait()
            return carry

        lax.fori_loop(0, n_sorted_tiles, start, 0)
        lax.fori_loop(0, n_sorted_tiles, wait, 0)

    def scatter(src_ref):
        def body(r, carry):
            p = pos_ref[i * TM + r]
            pltpu.make_async_copy(src_ref.at[r], xs_ref.at[p], sem).start()
            return carry
        lax.fori_loop(0, TM, body, 0)
        _row_copy_wait(src_ref, xs_ref.at[pl.ds(0, TM)], sem)

    @pl.when(i < n_prompt_tiles)
    def _():
        scatter(hp_ref)

    @pl.when(i >= n_prompt_tiles)
    def _():
        scatter(hs_ref)


def _dispatch(pos, zero_tiles, h2_p, h2_s):
    n_p = h2_p.shape[0] // TM
    n_s = h2_s.shape[0] // TM
    n_sorted_tiles = zero_tiles.shape[0]
    grid_spec = pltpu.PrefetchScalarGridSpec(
        num_scalar_prefetch=2,
        grid=(n_p + n_s,),
        in_specs=[
            pl.BlockSpec((TM, 1, DX), lambda i, pos, zt: (jnp.minimum(i, n_p - 1), 0, 0)),
            pl.BlockSpec((TM, 1, DX), lambda i, pos, zt: (jnp.maximum(i - n_p, 0), 0, 0)),
        ],
        out_specs=pl.BlockSpec(memory_space=pl.ANY),
        scratch_shapes=[pltpu.VMEM((TMM, 1, DX), jnp.float32),
                        pltpu.SemaphoreType.DMA, pltpu.SemaphoreType.DMA],
    )
    return pl.pallas_call(
        functools.partial(_dispatch_kernel, n_prompt_tiles=n_p, n_sorted_tiles=n_sorted_tiles),
        out_shape=jax.ShapeDtypeStruct((n_sorted_tiles * TMM, 1, DX), jnp.float32),
        grid_spec=grid_spec,
        compiler_params=pltpu.CompilerParams(
            dimension_semantics=("arbitrary",), vmem_limit_bytes=VMEM_LIMIT),
        name="dispatch",
    )(pos, zero_tiles, h2_p, h2_s)


def _split_into(w_ref, s_ref):
    n_rows = w_ref.shape[0]
    for c in range(0, n_rows, SPLIT_ROWS):
        hi, lo = _split(w_ref[c:c + SPLIT_ROWS, :])
        s_ref[0, c:c + SPLIT_ROWS, :] = hi
        s_ref[1, c:c + SPLIT_ROWS, :] = lo


def _expert_kernel(ea_ref, eb_ref, nu_ref, xs_ref, fga_ref, fua_ref, fda_ref, fgb_ref, fub_ref,
                   fdb_ref, ys_ref, x2d_ref, wga_ref, wua_ref, wda_ref, wgb_ref, wub_ref, wdb_ref):
    j = pl.program_id(0)
    prev = jnp.maximum(j - 1, 0)

    @pl.when((j == 0) | (ea_ref[j] != ea_ref[prev]))
    def _():
        _split_into(fga_ref, wga_ref)
        _split_into(fua_ref, wua_ref)
        _split_into(fda_ref, wda_ref)

    @pl.when((j == 0) | (eb_ref[j] != eb_ref[prev]))
    def _():
        _split_into(fgb_ref, wgb_ref)
        _split_into(fub_ref, wub_ref)
        _split_into(fdb_ref, wdb_ref)

    @pl.when(j < nu_ref[0])
    def _():
        x2d_ref[...] = xs_ref[...].reshape(TMM, DX)
        x_hi, x_lo = _split(x2d_ref[:, 0:D])
        gate_a = x2d_ref[:, D:D + 1]
        gate_b = x2d_ref[:, D + 1:D + 2]

        def expert(wg_ref, wu_ref, wd_ref, gate):
            hg = _dot3(x_hi, x_lo, wg_ref[0], wg_ref[1])
            hu = _dot3(x_hi, x_lo, wu_ref[0], wu_ref[1])
            hid_hi, hid_lo = _split(hg * jax.nn.sigmoid(hg) * hu * gate)
            return _dot3(hid_hi, hid_lo, wd_ref[0], wd_ref[1])

        y = expert(wga_ref, wua_ref, wda_ref, gate_a) + expert(wgb_ref, wub_ref, wdb_ref, gate_b)
        ys_ref[...] = y.reshape(TMM, 1, D)

    @pl.when(j >= nu_ref[0])
    def _():
        ys_ref[...] = jnp.zeros_like(ys_ref)


def _experts(ea, eb, n_used, xs, wg, wu, wd):
    n_tiles = xs.shape[0] // TMM

    def x_map(j, ea, eb, nu):
        return (jnp.minimum(j, nu[0] - 1), 0, 0)

    def a_map(j, ea, eb, nu):
        return (ea[j], 0, 0)

    def b_map(j, ea, eb, nu):
        return (eb[j], 0, 0)

    up = (None, D, D_FF)
    down = (None, D_FF, D)
    up_split = pltpu.VMEM((2, D, D_FF), jnp.bfloat16)
    down_split = pltpu.VMEM((2, D_FF, D), jnp.bfloat16)
    grid_spec = pltpu.PrefetchScalarGridSpec(
        num_scalar_prefetch=3,
        grid=(n_tiles,),
        in_specs=[
            pl.BlockSpec((TMM, 1, DX), x_map),
            pl.BlockSpec(up, a_map), pl.BlockSpec(up, a_map), pl.BlockSpec(down, a_map),
            pl.BlockSpec(up, b_map), pl.BlockSpec(up, b_map), pl.BlockSpec(down, b_map),
        ],
        out_specs=pl.BlockSpec((TMM, 1, D), lambda j, ea, eb, nu: (j, 0, 0)),
        scratch_shapes=[pltpu.VMEM((TMM, DX), jnp.float32),
                        up_split, up_split, down_split, up_split, up_split, down_split],
    )
    return pl.pallas_call(
        _expert_kernel,
        out_shape=jax.ShapeDtypeStruct((xs.shape[0], 1, D), jnp.float32),
        grid_spec=grid_spec,
        compiler_params=pltpu.CompilerParams(
            dimension_semantics=("arbitrary",), vmem_limit_bytes=VMEM_LIMIT),
        name="experts",
    )(ea, eb, n_used, xs, wg, wu, wd, wg, wu, wd)


def _combine_kernel(pos_ref, xp_ref, xs_ref, gp_ref, gs_ref, fg_ref, ys_ref, op_ref, os_ref,
                    g3_ref, g2_ref, sem, *, n_prompt_tiles, final):
    i = pl.program_id(0)

    def body(r, carry):
        p = pos_ref[i * TM + r]
        pltpu.make_async_copy(ys_ref.at[p], g3_ref.at[r], sem).start()
        return carry
    lax.fori_loop(0, TM, body, 0)
    _row_copy_wait(ys_ref.at[pl.ds(0, TM)], g3_ref, sem)
    g2_ref[...] = g3_ref[...].reshape(TM, D)

    def finish(x_ref, gate_ref, o_ref):
        x2 = _gated_add(x_ref[...], gate_ref[...], g2_ref[...])
        if final:
            ms = jnp.mean(x2 * x2, axis=-1, keepdims=True)
            x2 = x2 * lax.rsqrt(ms + EPS) * fg_ref[...]
        o_ref[...] = x2

    @pl.when(i < n_prompt_tiles)
    def _():
        finish(xp_ref, gp_ref, op_ref)

    @pl.when(i >= n_prompt_tiles)
    def _():
        finish(xs_ref, gs_ref, os_ref)


def _combine(pos, x1_p, x1_s, mod, final_g, ys, *, layer, final):
    n_p = x1_p.shape[0] // TM
    n_s = x1_s.shape[0] // TM

    def p_map(i, pos):
        return (jnp.minimum(i, n_p - 1), 0)

    def s_map(i, pos):
        return (jnp.maximum(i - n_p, 0), 0)

    grid_spec = pltpu.PrefetchScalarGridSpec(
        num_scalar_prefetch=1,
        grid=(n_p + n_s,),
        in_specs=[
            pl.BlockSpec((TM, D), p_map),
            pl.BlockSpec((TM, D), s_map),
            pl.BlockSpec((None, None, MOD_ROWS, D), lambda i, pos: (layer, 5, 0, 0)),
            pl.BlockSpec((None, None, MOD_ROWS, D), lambda i, pos: (layer, 5, 1, 0)),
            pl.BlockSpec((1, D), lambda i, pos: (0, 0)),
            pl.BlockSpec(memory_space=pl.ANY),
        ],
        out_specs=[pl.BlockSpec((TM, D), p_map), pl.BlockSpec((TM, D), s_map)],
        scratch_shapes=[
            pltpu.VMEM((TM, 1, D), jnp.float32),
            pltpu.VMEM((TM, D), jnp.float32),
            pltpu.SemaphoreType.DMA,
        ],
    )
    return pl.pallas_call(
        functools.partial(_combine_kernel, n_prompt_tiles=n_p, final=final),
        out_shape=[jax.ShapeDtypeStruct(x1_p.shape, jnp.float32),
                   jax.ShapeDtypeStruct(x1_s.shape, jnp.float32)],
        grid_spec=grid_spec,
        compiler_params=pltpu.CompilerParams(
            dimension_semantics=("arbitrary",), vmem_limit_bytes=VMEM_LIMIT),
        name="combine",
    )(pos, x1_p, x1_s, mod, mod, final_g, ys)


def _plan(route_p, route_s, cnt_p, cnt_s, n_sorted_tiles):
    b_p = route_p[:, 0, :].reshape(-1).astype(jnp.int32)
    r_p = route_p[:, 1, :].reshape(-1).astype(jnp.int32)
    b_s = route_s[:, 0, :].reshape(-1).astype(jnp.int32)
    r_s = route_s[:, 1, :].reshape(-1).astype(jnp.int32)
    c_p = cnt_p[:, 0].astype(jnp.int32)
    c_s = cnt_s[:, 0].astype(jnp.int32)
    tiles_per_bucket = (c_p + c_s + TMM - 1) // TMM
    tile_end = jnp.cumsum(tiles_per_bucket)
    row_start = (tile_end - tiles_per_bucket) * TMM
    pos = jnp.concatenate([row_start[b_p] + r_p, row_start[b_s] + c_p[b_s] + r_s])
    n_used = tile_end[-1]
    tile = jnp.minimum(jnp.arange(n_sorted_tiles, dtype=jnp.int32), n_used - 1)
    tile_bucket = jnp.sum((tile[:, None] >= tile_end[None, :]).astype(jnp.int32), axis=1)
    group = tile_bucket // 6
    pair = tile_bucket % 6
    ea = GROUP_SIZE * group + jnp.asarray(_PAIR_A, jnp.int32)[pair]
    eb = GROUP_SIZE * group + jnp.asarray(_PAIR_B, jnp.int32)[pair]
    all_tiles = jnp.arange(n_sorted_tiles, dtype=jnp.int32)
    last_of_bucket = jnp.any((all_tiles[:, None] == tile_end[None, :] - 1)
                             & (tiles_per_bucket[None, :] > 0), axis=1)
    zero_tiles = (last_of_bucket | (all_tiles >= n_used)).astype(jnp.int32)
    return pos, ea, eb, n_used.reshape(1), zero_tiles


def _time_major(x):
    b, s, d = x.shape
    return jnp.transpose(x, (1, 0, 2)).reshape(s * b, d)


def _batch_major(x, batch):
    rows, d = x.shape
    return jnp.transpose(x.reshape(rows // batch, batch, d), (1, 0, 2))


def kernel(x_prompt, x_sample, c_prompt, c_sample, state_pool_buf, state_conv_buf, state_lru_h,
           w_mod, b_mod, norm_g, pool_w, pool_scale, lru_w_x, lru_b_x, lru_w_y, lru_b_y,
           conv_w, conv_b, lru_w_a, lru_b_a, lru_w_i, lru_b_i, lru_lambda, lru_w_out, lru_b_out,
           router_w, router_b, moe_w_gate, moe_w_up, moe_w_down, final_g):
    bp, sp, _ = x_prompt.shape
    bs, ss, _ = x_sample.shape
    f32 = jnp.float32

    c_exp = jnp.concatenate([jnp.tile(c_prompt, (MOD_ROWS // bp, 1)),
                             jnp.tile(c_sample, (MOD_ROWS // bs, 1))], axis=0)
    mod = _modulation(c_exp, w_mod, b_mod)
    norm_g4 = norm_g.reshape(DEPTH, 2, 1, D)
    rw_t = router_w.T
    rb = router_b.reshape(N_EXPERTS, 1)
    fg = final_g.reshape(1, D)

    xp = _time_major(x_prompt)
    xs = _time_major(x_sample)
    n_tokens = xp.shape[0] + xs.shape[0]
    n_sorted_tiles = n_tokens // TMM + N_BUCKETS

    groups = (
        dict(batch=bp, start=0, group=0),
        dict(batch=bs, start=PAST_LEN, group=1),
    )
    new_pool = ([], [])
    new_conv = ([], [])
    new_h = ([], [])
    x = [xp, xs]
    for layer in range(DEPTH):
        j = layer // 2
        outs = []
        for gi, g in enumerate(groups):
            b = g["batch"]
            if layer % 2 == 0:
                if gi == 0:
                    buf = jnp.zeros((POOL_BUF * b, D), f32)
                else:
                    buf = _time_major(state_pool_buf[j])
                x1, h2, route, cnt, nbuf = _pool_layer(
                    x[gi], mod, norm_g4, buf, _split_weight(pool_w[j]), pool_scale[j].reshape(1, D),
                    rw_t, rb, layer=layer, **g)
                new_pool[gi].append(_batch_major(nbuf, b))
            else:
                if gi == 0:
                    cbuf = jnp.zeros(((CONV_WIDTH - 1) * b, D), f32)
                    h0 = jnp.zeros((b, D), f32)
                else:
                    cbuf = _time_major(state_conv_buf[j])
                    h0 = state_lru_h[j]
                p = dict(
                    w_x=_split_weight(lru_w_x[j]), b_x=lru_b_x[j].reshape(1, D),
                    w_y=_split_weight(lru_w_y[j]), b_y=lru_b_y[j].reshape(1, D),
                    cw=conv_w[j], cb=conv_b[j].reshape(1, D),
                    w_a=_split_weight(lru_w_a[j]), b_a=lru_b_a[j].reshape(1, D),
                    w_i=_split_weight(lru_w_i[j]), b_i=lru_b_i[j].reshape(1, D),
                    lam=lru_lambda[j].reshape(1, D),
                    w_out=_split_weight(lru_w_out[j]), b_out=lru_b_out[j].reshape(1, D))
                x1, h2, route, cnt, nconv, nh = _lru_layer(
                    x[gi], mod, norm_g4, cbuf, h0, p, rw_t, rb, layer=layer, **g)
                new_conv[gi].append(_batch_major(nconv, b))
                new_h[gi].append(nh)
            outs.append((x1, h2, route, cnt))
        (x1_p, h2_p, route_p, cnt_p), (x1_s, h2_s, route_s, cnt_s) = outs
        pos, ea, eb, n_used, zero_tiles = _plan(route_p, route_s, cnt_p, cnt_s, n_sorted_tiles)
        xsorted = _dispatch(pos, zero_tiles, h2_p, h2_s)
        ysorted = _experts(ea, eb, n_used, xsorted,
                           moe_w_gate[layer], moe_w_up[layer], moe_w_down[layer])
        x = list(_combine(pos, x1_p, x1_s, mod, fg, ysorted, layer=layer,
                          final=(layer == DEPTH - 1)))

    y_prompt = _batch_major(x[0], bp)
    y_sample = _batch_major(x[1], bs)
    return (y_prompt, y_sample,
            jnp.stack(new_pool[0]), jnp.stack(new_pool[1]),
            jnp.stack(new_conv[0]), jnp.stack(new_conv[1]),
            jnp.stack(new_h[0]), jnp.stack(new_h[1]))
```

```python
import functools

import jax
import jax.numpy as jnp
from jax import lax
from jax.experimental import pallas as pl
from jax.experimental.pallas import tpu as pltpu

D = 1024
DEPTH = 4
PAST_LEN = 16384
POOL_WINDOWS = (2, 4, 8, 16)
POOL_GW = 256
POOL_BUF = 15
LRU_BLOCKS = 4
LRU_BW = 256
CONV_WIDTH = 4
LRU_C = 8.0
N_EXPERTS = 16
GROUP_SIZE = 4
D_FF = 512
EPS = 1e-6

MOD_ROWS = 128
TM = 512
TM_MIX = 256
TMM = 256
SPLIT_ROWS = 128
ROW_UNROLL = 8
GATE_LANES = 128
DX = D + GATE_LANES
N_BUCKETS = 24
BUCKET_ROWS = 32
VMEM_LIMIT = 56 * 1024 * 1024

_PAIR_A = (0, 0, 0, 1, 1, 2)
_PAIR_B = (1, 2, 3, 2, 3, 3)


def _bf(x):
    return x.astype(jnp.bfloat16)


def _split(x):
    hi = _bf(x)
    lo = _bf(x - hi.astype(jnp.float32))
    return hi, lo


def _split_weight(w):
    bits = lax.bitcast_convert_type(w, jnp.uint32) & jnp.uint32(0xFFFF0000)
    hi = lax.bitcast_convert_type(bits, jnp.float32)
    return jnp.stack([_bf(hi), _bf(w - hi)], axis=0)


def _dot(a, b):
    return jnp.dot(a, b, preferred_element_type=jnp.float32)


def _dot3(a_hi, a_lo, w_hi, w_lo):
    rows = a_hi.shape[0]
    both = _dot(jnp.concatenate([a_hi, a_lo], axis=0), w_hi)
    return both[0:rows] + both[rows:2 * rows] + _dot(a_hi, w_lo)


def _dot_nt(a, b):
    return lax.dot_general(a, b, (((1,), (1,)), ((), ())), preferred_element_type=jnp.float32)


def _mod_kernel(c_ref, w_ref, b_ref, o_ref):
    c = c_ref[...]
    cs = c * jax.nn.sigmoid(c)
    c_hi, c_lo = _split(cs)
    w_hi, w_lo = _split(w_ref[...])
    o_ref[...] = _dot(c_hi, w_hi) + _dot(c_lo, w_hi) + _dot(c_hi, w_lo) + b_ref[...]


def _modulation(c_exp, w_mod, b_mod):
    rows = c_exp.shape[0]
    return pl.pallas_call(
        _mod_kernel,
        out_shape=jax.ShapeDtypeStruct((DEPTH, 6, rows, D), jnp.float32),
        grid=(DEPTH, 6),
        in_specs=[
            pl.BlockSpec((rows, D), lambda i, n: (0, 0)),
            pl.BlockSpec((None, D, D), lambda i, n: (i, 0, n)),
            pl.BlockSpec((None, None, 1, D), lambda i, n: (i, n, 0, 0)),
        ],
        out_specs=pl.BlockSpec((None, None, rows, D), lambda i, n: (i, n, 0, 0)),
        compiler_params=pltpu.CompilerParams(
            dimension_semantics=("arbitrary", "arbitrary"), vmem_limit_bytes=VMEM_LIMIT),
        name="modulation",
    )(c_exp, w_mod, b_mod.reshape(DEPTH, 6, 1, D))


def _norm_mod(x, gamma, shift, scale):
    rows = x.shape[0]
    ms = jnp.mean(x * x, axis=-1, keepdims=True)
    y = x * lax.rsqrt(ms + EPS) * gamma
    y3 = y.reshape(rows // MOD_ROWS, MOD_ROWS, D)
    return (y3 * (1.0 + scale)[None] + shift[None]).reshape(rows, D)


def _gated_add(x, gate, y):
    rows = x.shape[0]
    y3 = y.reshape(rows // MOD_ROWS, MOD_ROWS, D)
    return x + (y3 * gate[None]).reshape(rows, D)


def _route(h2, rw_ref, rb_ref, run_ref):
    rows = h2.shape[0]
    h_hi, h_lo = _split(h2)
    w_hi, w_lo = _split(rw_ref[...])
    logits = _dot_nt(w_hi, h_hi) + _dot_nt(w_lo, h_hi) + _dot_nt(w_hi, h_lo)
    m = jnp.max(logits, axis=0, keepdims=True)
    e = jnp.exp(logits - m)
    scores = e / jnp.sum(e, axis=0, keepdims=True)
    sel = scores + rb_ref[...]
    eidx = lax.broadcasted_iota(jnp.int32, (N_EXPERTS, rows), 0).astype(jnp.float32)
    gidx = jnp.floor(eidx * (1.0 / GROUP_SIZE))
    neg = -jnp.inf
    sentinel = float(N_EXPERTS)

    def top2(vals):
        v1 = jnp.max(vals, axis=0, keepdims=True)
        i1 = jnp.min(jnp.where(vals == v1, eidx, sentinel), axis=0, keepdims=True)
        rest = jnp.where(eidx == i1, neg, vals)
        v2 = jnp.max(rest, axis=0, keepdims=True)
        i2 = jnp.min(jnp.where(rest == v2, eidx, sentinel), axis=0, keepdims=True)
        return v1, i1, v2, i2

    best = jnp.zeros((1, rows), jnp.float32)
    best_v = None
    for g in range(N_EXPERTS // GROUP_SIZE):
        v1, _, v2, _ = top2(jnp.where(gidx == float(g), sel, neg))
        gs = v1 + v2
        if best_v is None:
            best_v = gs
        else:
            upd = gs > best_v
            best = jnp.where(upd, float(g), best)
            best_v = jnp.where(upd, gs, best_v)
    _, i1, _, i2 = top2(jnp.where(gidx == best, sel, neg))
    chosen = (eidx == i1) | (eidx == i2)
    ssum = jnp.sum(jnp.where(chosen, scores, 0.0), axis=0, keepdims=True)
    gates = jnp.where(chosen, scores / ssum, 0.0)
    ea = jnp.minimum(i1, i2)
    eb = jnp.maximum(i1, i2)
    wa = jnp.sum(jnp.where(eidx == ea, gates, 0.0), axis=0, keepdims=True)
    wb = jnp.sum(jnp.where(eidx == eb, gates, 0.0), axis=0, keepdims=True)
    a = ea - GROUP_SIZE * best
    b = eb - GROUP_SIZE * best
    pair = jnp.where(a == 0.0, b - 1.0, jnp.where(a == 1.0, b + 1.0, 5.0))
    bucket = 6.0 * best + pair

    bidx = lax.broadcasted_iota(jnp.int32, (BUCKET_ROWS, rows), 0).astype(jnp.float32)
    onehot = jnp.where(bidx == bucket, 1.0, 0.0)
    src = lax.broadcasted_iota(jnp.int32, (rows, rows), 0)
    dst = lax.broadcasted_iota(jnp.int32, (rows, rows), 1)
    before = _bf(jnp.where(src < dst, 1.0, 0.0))
    earlier = _dot(_bf(onehot), before)
    base = run_ref[:, 0:1]
    rank = jnp.sum(onehot * (earlier + base), axis=0, keepdims=True)
    run_ref[...] = run_ref[...] + jnp.sum(onehot, axis=1, keepdims=True)
    return bucket, rank, wa, wb


def _emit_routed(h2, bucket, rank, wa, wb, h2_ref, route_ref, cnt_ref, run_ref):
    rows = h2.shape[0]
    lane_row = lax.broadcasted_iota(jnp.int32, (GATE_LANES, rows), 0)
    gate_t = jnp.where(lane_row == 0, wa, jnp.where(lane_row == 1, wb, 0.0))
    h2_ref[:, :, 0:D] = h2.reshape(rows, 1, D)
    h2_ref[:, :, D:DX] = gate_t.T.reshape(rows, 1, GATE_LANES)
    info_row = lax.broadcasted_iota(jnp.int32, (8, rows), 0)
    route_ref[...] = jnp.where(info_row == 0, bucket, jnp.where(info_row == 1, rank, 0.0))
    cnt_ref[...] = run_ref[...]


def _time_index(i, rows, batch, steps, start):
    r = lax.broadcasted_iota(jnp.int32, (rows, 1), 0)
    return start + i * steps + r // batch


def _pool_kernel(x_ref, sh1_ref, sc1_ref, g1_ref, sh2_ref, sc2_ref, n1_ref, n2_ref, buf_ref,
                 wp_ref, ps_ref, rw_ref, rb_ref,
                 x1_ref, h2_ref, route_ref, cnt_ref, nbuf_ref,
                 hb_ref, run_ref, *, batch, steps, n_tiles, start):
    i = pl.program_id(0)
    rows = steps * batch
    halo = POOL_BUF * batch

    @pl.when(i == 0)
    def _():
        hb_ref[0:halo, :] = buf_ref[...]
        run_ref[...] = jnp.zeros_like(run_ref)

    x = x_ref[...]
    h = _norm_mod(x, n1_ref[...], sh1_ref[...], sc1_ref[...])
    hb_ref[halo:halo + rows, :] = h
    pos = _time_index(i, rows, batch, steps, start)
    mixed = []
    for g, w in enumerate(POOL_WINDOWS):
        c0, c1 = g * POOL_GW, (g + 1) * POOL_GW
        acc = hb_ref[halo:halo + rows, c0:c1]
        for k in range(1, w):
            off = (POOL_BUF - k) * batch
            acc = acc + hb_ref[off:off + rows, c0:c1]
        inv_cnt = 1.0 / jnp.minimum(pos + 1, w).astype(jnp.float32)
        pooled = acc * inv_cnt - h[:, c0:c1]
        p_hi, p_lo = _split(pooled)
        mixed.append(_dot3(p_hi, p_lo, wp_ref[0, g], wp_ref[1, g]))
    mix = jnp.concatenate(mixed, axis=-1) * ps_ref[...]
    x1 = _gated_add(x, g1_ref[...], mix)
    x1_ref[...] = x1

    if n_tiles > 1:
        @pl.when(i < n_tiles - 1)
        def _():
            for k in range(POOL_BUF):
                hb_ref[k * batch:(k + 1) * batch, :] = hb_ref[rows + k * batch:rows + (k + 1) * batch, :]

    @pl.when(i == n_tiles - 1)
    def _():
        nbuf_ref[...] = hb_ref[rows:rows + halo, :]

    h2 = _norm_mod(x1, n2_ref[...], sh2_ref[...], sc2_ref[...])
    bucket, rank, wa, wb = _route(h2, rw_ref, rb_ref, run_ref)
    _emit_routed(h2, bucket, rank, wa, wb, h2_ref, route_ref, cnt_ref, run_ref)


def _mod_spec(layer, k, group):
    return pl.BlockSpec((None, None, MOD_ROWS, D), lambda i: (layer, k, group, 0))


def _norm_spec(layer, k):
    return pl.BlockSpec((None, None, 1, D), lambda i: (layer, k, 0, 0))


def _full_spec(shape):
    nd = len(shape)
    return pl.BlockSpec(shape, lambda i: (0,) * nd)


def _const_spec(shape):
    nd = len(shape)
    return pl.BlockSpec(shape, lambda i: (0,) * nd, pipeline_mode=pl.Buffered(1))


def _mixer_out(rows_total, n_tiles, rows):
    out_shape = [
        jax.ShapeDtypeStruct((rows_total, D), jnp.float32),
        jax.ShapeDtypeStruct((rows_total, 1, DX), jnp.float32),
        jax.ShapeDtypeStruct((n_tiles, 8, rows), jnp.float32),
        jax.ShapeDtypeStruct((BUCKET_ROWS, 128), jnp.float32),
    ]
    out_specs = [
        pl.BlockSpec((rows, D), lambda i: (i, 0)),
        pl.BlockSpec((rows, 1, DX), lambda i: (i, 0, 0)),
        pl.BlockSpec((None, 8, rows), lambda i: (i, 0, 0)),
        _full_spec((BUCKET_ROWS, 128)),
    ]
    return out_shape, out_specs


def _pool_layer(x, mod, norm_g4, buf, wp, ps, rw_t, rb, *, layer, group, batch, start):
    rows_total = x.shape[0]
    rows = min(TM_MIX, rows_total)
    steps = rows // batch
    n_tiles = rows_total // rows
    halo = POOL_BUF * batch
    out_shape, out_specs = _mixer_out(rows_total, n_tiles, rows)
    out_shape.append(jax.ShapeDtypeStruct((halo, D), jnp.float32))
    out_specs.append(_full_spec((halo, D)))
    kern = functools.partial(_pool_kernel, batch=batch, steps=steps, n_tiles=n_tiles, start=start)
    return pl.pallas_call(
        kern,
        out_shape=out_shape,
        grid=(n_tiles,),
        in_specs=[
            pl.BlockSpec((rows, D), lambda i: (i, 0)),
            _mod_spec(layer, 0, group), _mod_spec(layer, 1, group), _mod_spec(layer, 2, group),
            _mod_spec(layer, 3, group), _mod_spec(layer, 4, group),
            _norm_spec(layer, 0), _norm_spec(layer, 1),
            _full_spec((halo, D)),
            _full_spec((2, LRU_BLOCKS, POOL_GW, POOL_GW)),
            _full_spec((1, D)),
            _full_spec((N_EXPERTS, D)),
            _full_spec((N_EXPERTS, 1)),
        ],
        out_specs=out_specs,
        scratch_shapes=[
            pltpu.VMEM((halo + rows, D), jnp.float32),
            pltpu.VMEM((BUCKET_ROWS, 128), jnp.float32),
        ],
        compiler_params=pltpu.CompilerParams(
            dimension_semantics=("arbitrary",), vmem_limit_bytes=VMEM_LIMIT),
        name=f"pool_layer_b{batch}",
    )(x, mod, mod, mod, mod, mod, norm_g4, norm_g4, buf, wp, ps, rw_t, rb)


def _lru_kernel(x_ref, sh1_ref, sc1_ref, g1_ref, sh2_ref, sc2_ref, n1_ref, n2_ref,
                cbuf_ref, h0_ref, wx_ref, bx_ref, wy_ref, by_ref, cw_ref, cb_ref,
                wa_ref, ba_ref, wi_ref, bi_ref, lam_ref, wo_ref, bo_ref, rw_ref, rb_ref,
                x1_ref, h2_ref, route_ref, cnt_ref, nconv_ref, nh_ref,
                xp_ref, a_ref, b_ref, hs_ref, hst_ref, run_ref, *, batch, steps, n_tiles, start):
    i = pl.program_id(0)
    rows = steps * batch
    halo = (CONV_WIDTH - 1) * batch

    @pl.when(i == 0)
    def _():
        xp_ref[0:halo, :] = cbuf_ref[...]
        hst_ref[...] = h0_ref[...]
        run_ref[...] = jnp.zeros_like(run_ref)

    x = x_ref[...]
    h_hi, h_lo = _split(_norm_mod(x, n1_ref[...], sh1_ref[...], sc1_ref[...]))
    xb = _dot3(h_hi, h_lo, wx_ref[0], wx_ref[1]) + bx_ref[...]
    yb = jax.nn.gelu(_dot3(h_hi, h_lo, wy_ref[0], wy_ref[1]) + by_ref[...], approximate=True)
    xp_ref[halo:halo + rows, :] = xb
    xc = xp_ref[0:rows, :] * cw_ref[0:1, :]
    for k in range(1, CONV_WIDTH):
        xc = xc + xp_ref[k * batch:k * batch + rows, :] * cw_ref[k:k + 1, :]
    xc = xc + cb_ref[...]
    xc_hi, xc_lo = _split(xc)
    ra, ri = [], []
    for n in range(LRU_BLOCKS):
        cols = slice(n * LRU_BW, (n + 1) * LRU_BW)
        ra.append(_dot3(xc_hi[:, cols], xc_lo[:, cols], wa_ref[0, n], wa_ref[1, n]))
        ri.append(_dot3(xc_hi[:, cols], xc_lo[:, cols], wi_ref[0, n], wi_ref[1, n]))
    r = jax.nn.sigmoid(jnp.concatenate(ra, axis=-1) + ba_ref[...])
    gi = jax.nn.sigmoid(jnp.concatenate(ri, axis=-1) + bi_ref[...])
    neg_lam = -lam_ref[...]
    softplus = jnp.maximum(neg_lam, 0.0) + jnp.log1p(jnp.exp(-jnp.abs(neg_lam)))
    log_a = -LRU_C * r * softplus
    a = jnp.exp(log_a)
    mult = jnp.sqrt(jnp.maximum(1.0 - jnp.exp(2.0 * log_a), 0.0))
    pos = _time_index(i, rows, batch, steps, start)
    mult = jnp.where(pos == 0, 1.0, mult)
    a_ref[...] = a
    b_ref[...] = xc * gi * mult

    hcur = hst_ref[...]
    for t in range(steps):
        sl = slice(t * batch, (t + 1) * batch)
        hcur = a_ref[sl, :] * hcur + b_ref[sl, :]
        hs_ref[sl, :] = hcur
    hst_ref[...] = hcur

    gated_hi, gated_lo = _split(hs_ref[...] * yb)
    y = _dot3(gated_hi, gated_lo, wo_ref[0], wo_ref[1]) + bo_ref[...]
    x1 = _gated_add(x, g1_ref[...], y)
    x1_ref[...] = x1

    if n_tiles > 1:
        @pl.when(i < n_tiles - 1)
        def _():
            for k in range(CONV_WIDTH - 1):
                xp_ref[k * batch:(k + 1) * batch, :] = xp_ref[rows + k * batch:rows + (k + 1) * batch, :]

    @pl.when(i == n_tiles - 1)
    def _():
        nconv_ref[...] = xp_ref[rows:rows + halo, :]
        nh_ref[...] = hcur

    h2 = _norm_mod(x1, n2_ref[...], sh2_ref[...], sc2_ref[...])
    bucket, rank, wa, wb = _route(h2, rw_ref, rb_ref, run_ref)
    _emit_routed(h2, bucket, rank, wa, wb, h2_ref, route_ref, cnt_ref, run_ref)


def _lru_layer(x, mod, norm_g4, cbuf, h0, p, rw_t, rb, *, layer, group, batch, start):
    rows_total = x.shape[0]
    rows = min(TM_MIX, rows_total)
    steps = rows // batch
    n_tiles = rows_total // rows
    halo = (CONV_WIDTH - 1) * batch
    out_shape, out_specs = _mixer_out(rows_total, n_tiles, rows)
    out_shape += [jax.ShapeDtypeStruct((halo, D), jnp.float32),
                  jax.ShapeDtypeStruct((batch, D), jnp.float32)]
    out_specs += [_full_spec((halo, D)), _full_spec((batch, D))]
    kern = functools.partial(_lru_kernel, batch=batch, steps=steps, n_tiles=n_tiles, start=start)
    vec = _full_spec((1, D))
    sq = _const_spec((2, D, D))
    blk = _const_spec((2, LRU_BLOCKS, LRU_BW, LRU_BW))
    return pl.pallas_call(
        kern,
        out_shape=out_shape,
        grid=(n_tiles,),
        in_specs=[
            pl.BlockSpec((rows, D), lambda i: (i, 0)),
            _mod_spec(layer, 0, group), _mod_spec(layer, 1, group), _mod_spec(layer, 2, group),
            _mod_spec(layer, 3, group), _mod_spec(layer, 4, group),
            _norm_spec(layer, 0), _norm_spec(layer, 1),
            _full_spec((halo, D)), _full_spec((batch, D)),
            sq, vec, sq, vec, _full_spec((CONV_WIDTH, D)), vec,
            blk, vec, blk, vec, vec, sq, vec,
            _full_spec((N_EXPERTS, D)), _full_spec((N_EXPERTS, 1)),
        ],
        out_specs=out_specs,
        scratch_shapes=[
            pltpu.VMEM((halo + rows, D), jnp.float32),
            pltpu.VMEM((rows, D), jnp.float32),
            pltpu.VMEM((rows, D), jnp.float32),
            pltpu.VMEM((rows, D), jnp.float32),
            pltpu.VMEM((batch, D), jnp.float32),
            pltpu.VMEM((BUCKET_ROWS, 128), jnp.float32),
        ],
        compiler_params=pltpu.CompilerParams(
            dimension_semantics=("arbitrary",), vmem_limit_bytes=VMEM_LIMIT),
        name=f"lru_layer_b{batch}",
    )(x, mod, mod, mod, mod, mod, norm_g4, norm_g4, cbuf, h0,
      p["w_x"], p["b_x"], p["w_y"], p["b_y"], p["cw"], p["cb"],
      p["w_a"], p["b_a"], p["w_i"], p["b_i"], p["lam"], p["w_out"], p["b_out"], rw_t, rb)


def _row_copy_wait(src_ref, dst_ref, sem):
    pltpu.make_async_copy(src_ref, dst_ref, sem).wait()


def _dispatch_kernel(pos_ref, zt_ref, hp_ref, hs_ref, xs_ref, zero_ref, sem, zsem, *,
                     n_prompt_tiles, n_sorted_tiles):
    i = pl.program_id(0)

    @pl.when(i == 0)
    def _():
        zero_ref[...] = jnp.zeros_like(zero_ref)

        def fill(t):
            return pltpu.make_async_copy(zero_ref, xs_ref.at[pl.ds(t * TMM, TMM)], zsem)

        def start(t, carry):
            @pl.when(zt_ref[t] != 0)
            def _():
                fill(t).start()
            return carry

        def wait(t, carry):
            @pl.when(zt_ref[t] != 0)
            def _():
                fill(t).wait()
            return carry

        lax.fori_loop(0, n_sorted_tiles, start, 0)
        lax.fori_loop(0, n_sorted_tiles, wait, 0)

    def scatter(src_ref):
        def body(c, carry):
            for u in range(ROW_UNROLL):
                r = c * ROW_UNROLL + u
                p = pos_ref[i * TM + r]
                pltpu.make_async_copy(src_ref.at[r], xs_ref.at[p], sem).start(priority=u % 2)
            return carry
        lax.fori_loop(0, TM // ROW_UNROLL, body, 0)
        _row_copy_wait(src_ref, xs_ref.at[pl.ds(0, TM)], sem)

    @pl.when(i < n_prompt_tiles)
    def _():
        scatter(hp_ref)

    @pl.when(i >= n_prompt_tiles)
    def _():
        scatter(hs_ref)


def _dispatch(pos, zero_tiles, h2_p, h2_s):
    n_p = h2_p.shape[0] // TM
    n_s = h2_s.shape[0] // TM
    n_sorted_tiles = zero_tiles.shape[0]
    grid_spec = pltpu.PrefetchScalarGridSpec(
        num_scalar_prefetch=2,
        grid=(n_p + n_s,),
        in_specs=[
            pl.BlockSpec((TM, 1, DX), lambda i, pos, zt: (jnp.minimum(i, n_p - 1), 0, 0)),
            pl.BlockSpec((TM, 1, DX), lambda i, pos, zt: (jnp.maximum(i - n_p, 0), 0, 0)),
        ],
        out_specs=pl.BlockSpec(memory_space=pl.ANY),
        scratch_shapes=[pltpu.VMEM((TMM, 1, DX), jnp.float32),
                        pltpu.SemaphoreType.DMA, pltpu.SemaphoreType.DMA],
    )
    return pl.pallas_call(
        functools.partial(_dispatch_kernel, n_prompt_tiles=n_p, n_sorted_tiles=n_sorted_tiles),
        out_shape=jax.ShapeDtypeStruct((n_sorted_tiles * TMM, 1, DX), jnp.float32),
        grid_spec=grid_spec,
        compiler_params=pltpu.CompilerParams(
            dimension_semantics=("arbitrary",), vmem_limit_bytes=VMEM_LIMIT),
        name="dispatch",
    )(pos, zero_tiles, h2_p, h2_s)


def _split_into(w_ref, s_ref):
    n_rows = w_ref.shape[0]
    for c in range(0, n_rows, SPLIT_ROWS):
        hi, lo = _split(w_ref[c:c + SPLIT_ROWS, :])
        s_ref[0, c:c + SPLIT_ROWS, :] = hi
        s_ref[1, c:c + SPLIT_ROWS, :] = lo


def _expert_kernel(ea_ref, eb_ref, nu_ref, xs_ref, fga_ref, fua_ref, fda_ref, fgb_ref, fub_ref,
                   fdb_ref, ys_ref, x2d_ref, wga_ref, wua_ref, wda_ref, wgb_ref, wub_ref, wdb_ref):
    j = pl.program_id(0)
    prev = jnp.maximum(j - 1, 0)

    @pl.when((j == 0) | (ea_ref[j] != ea_ref[prev]))
    def _():
        _split_into(fga_ref, wga_ref)
        _split_into(fua_ref, wua_ref)
        _split_into(fda_ref, wda_ref)

    @pl.when((j == 0) | (eb_ref[j] != eb_ref[prev]))
    def _():
        _split_into(fgb_ref, wgb_ref)
        _split_into(fub_ref, wub_ref)
        _split_into(fdb_ref, wdb_ref)

    @pl.when(j < nu_ref[0])
    def _():
        x2d_ref[...] = xs_ref[...].reshape(TMM, DX)
        x_hi, x_lo = _split(x2d_ref[:, 0:D])
        gate_a = x2d_ref[:, D:D + 1]
        gate_b = x2d_ref[:, D + 1:D + 2]

        def expert(wg_ref, wu_ref, wd_ref, gate):
            hg = _dot3(x_hi, x_lo, wg_ref[0], wg_ref[1])
            hu = _dot3(x_hi, x_lo, wu_ref[0], wu_ref[1])
            hid_hi, hid_lo = _split(hg * jax.nn.sigmoid(hg) * hu * gate)
            return _dot3(hid_hi, hid_lo, wd_ref[0], wd_ref[1])

        y = expert(wga_ref, wua_ref, wda_ref, gate_a) + expert(wgb_ref, wub_ref, wdb_ref, gate_b)
        ys_ref[...] = y.reshape(TMM, 1, D)

    @pl.when(j >= nu_ref[0])
    def _():
        ys_ref[...] = jnp.zeros_like(ys_ref)


def _experts(ea, eb, n_used, xs, wg, wu, wd, *, layer):
    n_tiles = xs.shape[0] // TMM

    def x_map(j, ea, eb, nu):
        return (jnp.minimum(j, nu[0] - 1), 0, 0)

    def a_map(j, ea, eb, nu):
        return (layer, ea[j], 0, 0)

    def b_map(j, ea, eb, nu):
        return (layer, eb[j], 0, 0)

    up = (None, None, D, D_FF)
    down = (None, None, D_FF, D)
    up_split = pltpu.VMEM((2, D, D_FF), jnp.bfloat16)
    down_split = pltpu.VMEM((2, D_FF, D), jnp.bfloat16)
    grid_spec = pltpu.PrefetchScalarGridSpec(
        num_scalar_prefetch=3,
        grid=(n_tiles,),
        in_specs=[
            pl.BlockSpec((TMM, 1, DX), x_map),
            pl.BlockSpec(up, a_map), pl.BlockSpec(up, a_map), pl.BlockSpec(down, a_map),
            pl.BlockSpec(up, b_map), pl.BlockSpec(up, b_map), pl.BlockSpec(down, b_map),
        ],
        out_specs=pl.BlockSpec((TMM, 1, D), lambda j, ea, eb, nu: (j, 0, 0)),
        scratch_shapes=[pltpu.VMEM((TMM, DX), jnp.float32),
                        up_split, up_split, down_split, up_split, up_split, down_split],
    )
    return pl.pallas_call(
        _expert_kernel,
        out_shape=jax.ShapeDtypeStruct((xs.shape[0], 1, D), jnp.float32),
        grid_spec=grid_spec,
        compiler_params=pltpu.CompilerParams(
            dimension_semantics=("arbitrary",), vmem_limit_bytes=VMEM_LIMIT),
        name="experts",
    )(ea, eb, n_used, xs, wg, wu, wd, wg, wu, wd)


def _combine_kernel(pos_ref, xp_ref, xs_ref, gp_ref, gs_ref, fg_ref, ys_ref, op_ref, os_ref,
                    g3_ref, g2_ref, sem, *, n_prompt_tiles, final):
    i = pl.program_id(0)

    def body(c, carry):
        for u in range(ROW_UNROLL):
            r = c * ROW_UNROLL + u
            p = pos_ref[i * TM + r]
            pltpu.make_async_copy(ys_ref.at[p], g3_ref.at[r], sem).start(priority=u % 2)
        return carry
    lax.fori_loop(0, TM // ROW_UNROLL, body, 0)
    _row_copy_wait(ys_ref.at[pl.ds(0, TM)], g3_ref, sem)
    g2_ref[...] = g3_ref[...].reshape(TM, D)

    def finish(x_ref, gate_ref, o_ref):
        x2 = _gated_add(x_ref[...], gate_ref[...], g2_ref[...])
        if final:
            ms = jnp.mean(x2 * x2, axis=-1, keepdims=True)
            x2 = x2 * lax.rsqrt(ms + EPS) * fg_ref[...]
        o_ref[...] = x2

    @pl.when(i < n_prompt_tiles)
    def _():
        finish(xp_ref, gp_ref, op_ref)

    @pl.when(i >= n_prompt_tiles)
    def _():
        finish(xs_ref, gs_ref, os_ref)


def _combine(pos, x1_p, x1_s, mod, final_g, ys, *, layer, final):
    n_p = x1_p.shape[0] // TM
    n_s = x1_s.shape[0] // TM

    def p_map(i, pos):
        return (jnp.minimum(i, n_p - 1), 0)

    def s_map(i, pos):
        return (jnp.maximum(i - n_p, 0), 0)

    grid_spec = pltpu.PrefetchScalarGridSpec(
        num_scalar_prefetch=1,
        grid=(n_p + n_s,),
        in_specs=[
            pl.BlockSpec((TM, D), p_map),
            pl.BlockSpec((TM, D), s_map),
            pl.BlockSpec((None, None, MOD_ROWS, D), lambda i, pos: (layer, 5, 0, 0)),
            pl.BlockSpec((None, None, MOD_ROWS, D), lambda i, pos: (layer, 5, 1, 0)),
            pl.BlockSpec((1, D), lambda i, pos: (0, 0)),
            pl.BlockSpec(memory_space=pl.ANY),
        ],
        out_specs=[pl.BlockSpec((TM, D), p_map), pl.BlockSpec((TM, D), s_map)],
        scratch_shapes=[
            pltpu.VMEM((TM, 1, D), jnp.float32),
            pltpu.VMEM((TM, D), jnp.float32),
            pltpu.SemaphoreType.DMA,
        ],
    )
    return pl.pallas_call(
        functools.partial(_combine_kernel, n_prompt_tiles=n_p, final=final),
        out_shape=[jax.ShapeDtypeStruct(x1_p.shape, jnp.float32),
                   jax.ShapeDtypeStruct(x1_s.shape, jnp.float32)],
        grid_spec=grid_spec,
        compiler_params=pltpu.CompilerParams(
            dimension_semantics=("arbitrary",), vmem_limit_bytes=VMEM_LIMIT),
        name="combine",
    )(pos, x1_p, x1_s, mod, mod, final_g, ys)


def _plan(route_p, route_s, cnt_p, cnt_s, n_sorted_tiles):
    b_p = route_p[:, 0, :].reshape(-1).astype(jnp.int32)
    r_p = route_p[:, 1, :].reshape(-1).astype(jnp.int32)
    b_s = route_s[:, 0, :].reshape(-1).astype(jnp.int32)
    r_s = route_s[:, 1, :].reshape(-1).astype(jnp.int32)
    c_p = cnt_p[:, 0].astype(jnp.int32)
    c_s = cnt_s[:, 0].astype(jnp.int32)
    tiles_per_bucket = (c_p + c_s + TMM - 1) // TMM
    tile_end = jnp.cumsum(tiles_per_bucket)
    row_start = (tile_end - tiles_per_bucket) * TMM
    pos = jnp.concatenate([row_start[b_p] + r_p, row_start[b_s] + c_p[b_s] + r_s])
    n_used = tile_end[-1]
    tile = jnp.minimum(jnp.arange(n_sorted_tiles, dtype=jnp.int32), n_used - 1)
    tile_bucket = jnp.sum((tile[:, None] >= tile_end[None, :]).astype(jnp.int32), axis=1)
    group = tile_bucket // 6
    pair = tile_bucket % 6
    ea = GROUP_SIZE * group + jnp.asarray(_PAIR_A, jnp.int32)[pair]
    eb = GROUP_SIZE * group + jnp.asarray(_PAIR_B, jnp.int32)[pair]
    all_tiles = jnp.arange(n_sorted_tiles, dtype=jnp.int32)
    last_of_bucket = jnp.any((all_tiles[:, None] == tile_end[None, :] - 1)
                             & (tiles_per_bucket[None, :] > 0), axis=1)
    zero_tiles = (last_of_bucket | (all_tiles >= n_used)).astype(jnp.int32)
    return pos, ea, eb, n_used.reshape(1), zero_tiles


def _time_major(x):
    b, s, d = x.shape
    return jnp.transpose(x, (1, 0, 2)).reshape(s * b, d)


def _batch_major(x, batch):
    rows, d = x.shape
    return jnp.transpose(x.reshape(rows // batch, batch, d), (1, 0, 2))


def kernel(x_prompt, x_sample, c_prompt, c_sample, state_pool_buf, state_conv_buf, state_lru_h,
           w_mod, b_mod, norm_g, pool_w, pool_scale, lru_w_x, lru_b_x, lru_w_y, lru_b_y,
           conv_w, conv_b, lru_w_a, lru_b_a, lru_w_i, lru_b_i, lru_lambda, lru_w_out, lru_b_out,
           router_w, router_b, moe_w_gate, moe_w_up, moe_w_down, final_g):
    bp, sp, _ = x_prompt.shape
    bs, ss, _ = x_sample.shape
    f32 = jnp.float32

    c_exp = jnp.concatenate([jnp.tile(c_prompt, (MOD_ROWS // bp, 1)),
                             jnp.tile(c_sample, (MOD_ROWS // bs, 1))], axis=0)
    mod = _modulation(c_exp, w_mod, b_mod)
    norm_g4 = norm_g.reshape(DEPTH, 2, 1, D)
    rw_t = router_w.T
    rb = router_b.reshape(N_EXPERTS, 1)
    fg = final_g.reshape(1, D)

    xp = _time_major(x_prompt)
    xs = _time_major(x_sample)
    n_tokens = xp.shape[0] + xs.shape[0]
    n_sorted_tiles = n_tokens // TMM + N_BUCKETS

    groups = (
        dict(batch=bp, start=0, group=0),
        dict(batch=bs, start=PAST_LEN, group=1),
    )
    new_pool = ([], [])
    new_conv = ([], [])
    new_h = ([], [])
    x = [xp, xs]
    for layer in range(DEPTH):
        j = layer // 2
        outs = []
        for gi, g in enumerate(groups):
            b = g["batch"]
            if layer % 2 == 0:
                if gi == 0:
                    buf = jnp.zeros((POOL_BUF * b, D), f32)
                else:
                    buf = _time_major(state_pool_buf[j])
                x1, h2, route, cnt, nbuf = _pool_layer(
                    x[gi], mod, norm_g4, buf, _split_weight(pool_w[j]), pool_scale[j].reshape(1, D),
                    rw_t, rb, layer=layer, **g)
                new_pool[gi].append(_batch_major(nbuf, b))
            else:
                if gi == 0:
                    cbuf = jnp.zeros(((CONV_WIDTH - 1) * b, D), f32)
                    h0 = jnp.zeros((b, D), f32)
                else:
                    cbuf = _time_major(state_conv_buf[j])
                    h0 = state_lru_h[j]
                p = dict(
                    w_x=_split_weight(lru_w_x[j]), b_x=lru_b_x[j].reshape(1, D),
                    w_y=_split_weight(lru_w_y[j]), b_y=lru_b_y[j].reshape(1, D),
                    cw=conv_w[j], cb=conv_b[j].reshape(1, D),
                    w_a=_split_weight(lru_w_a[j]), b_a=lru_b_a[j].reshape(1, D),
                    w_i=_split_weight(lru_w_i[j]), b_i=lru_b_i[j].reshape(1, D),
                    lam=lru_lambda[j].reshape(1, D),
                    w_out=_split_weight(lru_w_out[j]), b_out=lru_b_out[j].reshape(1, D))
                x1, h2, route, cnt, nconv, nh = _lru_layer(
                    x[gi], mod, norm_g4, cbuf, h0, p, rw_t, rb, layer=layer, **g)
                new_conv[gi].append(_batch_major(nconv, b))
                new_h[gi].append(nh)
            outs.append((x1, h2, route, cnt))
        (x1_p, h2_p, route_p, cnt_p), (x1_s, h2_s, route_s, cnt_s) = outs
        pos, ea, eb, n_used, zero_tiles = _plan(route_p, route_s, cnt_p, cnt_s, n_sorted_tiles)
        xsorted = _dispatch(pos, zero_tiles, h2_p, h2_s)
        ysorted = _experts(ea, eb, n_used, xsorted, moe_w_gate, moe_w_up, moe_w_down, layer=layer)
        x = list(_combine(pos, x1_p, x1_s, mod, fg, ysorted, layer=layer,
                          final=(layer == DEPTH - 1)))

    y_prompt = _batch_major(x[0], bp)
    y_sample = _batch_major(x[1], bs)
    return (y_prompt, y_sample,
            jnp.stack(new_pool[0]), jnp.stack(new_pool[1]),
            jnp.stack(new_conv[0]), jnp.stack(new_conv[1]),
            jnp.stack(new_h[0]), jnp.stack(new_h[1]))
```

```python
import functools

import jax
import jax.numpy as jnp
from jax import lax
from jax.experimental import pallas as pl
from jax.experimental.pallas import tpu as pltpu

D = 1024
DEPTH = 4
PAST_LEN = 16384
POOL_WINDOWS = (2, 4, 8, 16)
POOL_GW = 256
POOL_BUF = 15
LRU_BLOCKS = 4
LRU_BW = 256
CONV_WIDTH = 4
LRU_C = 8.0
N_EXPERTS = 16
GROUP_SIZE = 4
D_FF = 512
EPS = 1e-6

MOD_ROWS = 128
TM = 512
TM_MIX = 512
TMM = 256
SPLIT_ROWS = 128
ROW_UNROLL = 8
GATE_LANES = 128
DX = D + GATE_LANES
N_BUCKETS = 24
BUCKET_ROWS = 32
VMEM_LIMIT = 56 * 1024 * 1024

_PAIR_A = (0, 0, 0, 1, 1, 2)
_PAIR_B = (1, 2, 3, 2, 3, 3)


def _bf(x):
    return x.astype(jnp.bfloat16)


def _split(x):
    hi = _bf(x)
    lo = _bf(x - hi.astype(jnp.float32))
    return hi, lo


def _split_weight(w):
    bits = lax.bitcast_convert_type(w, jnp.uint32) & jnp.uint32(0xFFFF0000)
    hi = lax.bitcast_convert_type(bits, jnp.float32)
    return jnp.stack([_bf(hi), _bf(w - hi)], axis=0)


def _dot(a, b):
    return jnp.dot(a, b, preferred_element_type=jnp.float32)


def _dot3(a_hi, a_lo, w_hi, w_lo):
    rows = a_hi.shape[0]
    both = _dot(jnp.concatenate([a_hi, a_lo], axis=0), w_hi)
    return both[0:rows] + both[rows:2 * rows] + _dot(a_hi, w_lo)


def _dot_nt(a, b):
    return lax.dot_general(a, b, (((1,), (1,)), ((), ())), preferred_element_type=jnp.float32)


def _mod_kernel(c_ref, w_ref, b_ref, o_ref):
    c = c_ref[...]
    cs = c * jax.nn.sigmoid(c)
    c_hi, c_lo = _split(cs)
    w_hi, w_lo = _split(w_ref[...])
    o_ref[...] = _dot(c_hi, w_hi) + _dot(c_lo, w_hi) + _dot(c_hi, w_lo) + b_ref[...]


def _modulation(c_exp, w_mod, b_mod):
    rows = c_exp.shape[0]
    return pl.pallas_call(
        _mod_kernel,
        out_shape=jax.ShapeDtypeStruct((DEPTH, 6, rows, D), jnp.float32),
        grid=(DEPTH, 6),
        in_specs=[
            pl.BlockSpec((rows, D), lambda i, n: (0, 0)),
            pl.BlockSpec((None, D, D), lambda i, n: (i, 0, n)),
            pl.BlockSpec((None, None, 1, D), lambda i, n: (i, n, 0, 0)),
        ],
        out_specs=pl.BlockSpec((None, None, rows, D), lambda i, n: (i, n, 0, 0)),
        compiler_params=pltpu.CompilerParams(
            dimension_semantics=("arbitrary", "arbitrary"), vmem_limit_bytes=VMEM_LIMIT),
        name="modulation",
    )(c_exp, w_mod, b_mod.reshape(DEPTH, 6, 1, D))


def _norm_mod(x, gamma, shift, scale):
    rows = x.shape[0]
    ms = jnp.mean(x * x, axis=-1, keepdims=True)
    y = x * lax.rsqrt(ms + EPS) * gamma
    y3 = y.reshape(rows // MOD_ROWS, MOD_ROWS, D)
    return (y3 * (1.0 + scale)[None] + shift[None]).reshape(rows, D)


def _gated_add(x, gate, y):
    rows = x.shape[0]
    y3 = y.reshape(rows // MOD_ROWS, MOD_ROWS, D)
    return x + (y3 * gate[None]).reshape(rows, D)


def _route(h2, rw_ref, rb_ref, run_ref):
    rows = h2.shape[0]
    h_hi, h_lo = _split(h2)
    w_hi, w_lo = _split(rw_ref[...])
    logits = _dot_nt(w_hi, h_hi) + _dot_nt(w_lo, h_hi) + _dot_nt(w_hi, h_lo)
    m = jnp.max(logits, axis=0, keepdims=True)
    e = jnp.exp(logits - m)
    scores = e / jnp.sum(e, axis=0, keepdims=True)
    sel = scores + rb_ref[...]
    eidx = lax.broadcasted_iota(jnp.int32, (N_EXPERTS, rows), 0).astype(jnp.float32)
    gidx = jnp.floor(eidx * (1.0 / GROUP_SIZE))
    neg = -jnp.inf
    sentinel = float(N_EXPERTS)

    def top2(vals):
        v1 = jnp.max(vals, axis=0, keepdims=True)
        i1 = jnp.min(jnp.where(vals == v1, eidx, sentinel), axis=0, keepdims=True)
        rest = jnp.where(eidx == i1, neg, vals)
        v2 = jnp.max(rest, axis=0, keepdims=True)
        i2 = jnp.min(jnp.where(rest == v2, eidx, sentinel), axis=0, keepdims=True)
        return v1, i1, v2, i2

    best = jnp.zeros((1, rows), jnp.float32)
    best_v = None
    for g in range(N_EXPERTS // GROUP_SIZE):
        v1, _, v2, _ = top2(jnp.where(gidx == float(g), sel, neg))
        gs = v1 + v2
        if best_v is None:
            best_v = gs
        else:
            upd = gs > best_v
            best = jnp.where(upd, float(g), best)
            best_v = jnp.where(upd, gs, best_v)
    _, i1, _, i2 = top2(jnp.where(gidx == best, sel, neg))
    chosen = (eidx == i1) | (eidx == i2)
    ssum = jnp.sum(jnp.where(chosen, scores, 0.0), axis=0, keepdims=True)
    gates = jnp.where(chosen, scores / ssum, 0.0)
    ea = jnp.minimum(i1, i2)
    eb = jnp.maximum(i1, i2)
    wa = jnp.sum(jnp.where(eidx == ea, gates, 0.0), axis=0, keepdims=True)
    wb = jnp.sum(jnp.where(eidx == eb, gates, 0.0), axis=0, keepdims=True)
    a = ea - GROUP_SIZE * best
    b = eb - GROUP_SIZE * best
    pair = jnp.where(a == 0.0, b - 1.0, jnp.where(a == 1.0, b + 1.0, 5.0))
    bucket = 6.0 * best + pair

    bidx = lax.broadcasted_iota(jnp.int32, (BUCKET_ROWS, rows), 0).astype(jnp.float32)
    onehot = jnp.where(bidx == bucket, 1.0, 0.0)
    src = lax.broadcasted_iota(jnp.int32, (rows, rows), 0)
    dst = lax.broadcasted_iota(jnp.int32, (rows, rows), 1)
    before = _bf(jnp.where(src < dst, 1.0, 0.0))
    earlier = _dot(_bf(onehot), before)
    base = run_ref[:, 0:1]
    rank = jnp.sum(onehot * (earlier + base), axis=0, keepdims=True)
    run_ref[...] = run_ref[...] + jnp.sum(onehot, axis=1, keepdims=True)
    return bucket, rank, wa, wb


def _emit_routed(h2, bucket, rank, wa, wb, h2_ref, route_ref, cnt_ref, run_ref):
    rows = h2.shape[0]
    lane_row = lax.broadcasted_iota(jnp.int32, (GATE_LANES, rows), 0)
    gate_t = jnp.where(lane_row == 0, wa, jnp.where(lane_row == 1, wb, 0.0))
    h2_ref[:, :, 0:D] = h2.reshape(rows, 1, D)
    h2_ref[:, :, D:DX] = gate_t.T.reshape(rows, 1, GATE_LANES)
    info_row = lax.broadcasted_iota(jnp.int32, (8, rows), 0)
    route_ref[...] = jnp.where(info_row == 0, bucket, jnp.where(info_row == 1, rank, 0.0))
    cnt_ref[...] = run_ref[...]


def _time_index(i, rows, batch, steps, start):
    r = lax.broadcasted_iota(jnp.int32, (rows, 1), 0)
    return start + i * steps + r // batch


def _pool_kernel(x_ref, sh1_ref, sc1_ref, g1_ref, sh2_ref, sc2_ref, n1_ref, n2_ref, buf_ref,
                 wp_ref, ps_ref, rw_ref, rb_ref,
                 x1_ref, h2_ref, route_ref, cnt_ref, nbuf_ref,
                 hb_ref, run_ref, *, batch, steps, n_tiles, start):
    i = pl.program_id(0)
    rows = steps * batch
    halo = POOL_BUF * batch

    @pl.when(i == 0)
    def _():
        hb_ref[0:halo, :] = buf_ref[...]
        run_ref[...] = jnp.zeros_like(run_ref)

    x = x_ref[...]
    h = _norm_mod(x, n1_ref[...], sh1_ref[...], sc1_ref[...])
    hb_ref[halo:halo + rows, :] = h
    pos = _time_index(i, rows, batch, steps, start)
    mixed = []
    for g, w in enumerate(POOL_WINDOWS):
        c0, c1 = g * POOL_GW, (g + 1) * POOL_GW
        acc = hb_ref[halo:halo + rows, c0:c1]
        for k in range(1, w):
            off = (POOL_BUF - k) * batch
            acc = acc + hb_ref[off:off + rows, c0:c1]
        inv_cnt = 1.0 / jnp.minimum(pos + 1, w).astype(jnp.float32)
        pooled = acc * inv_cnt - h[:, c0:c1]
        p_hi, p_lo = _split(pooled)
        mixed.append(_dot3(p_hi, p_lo, wp_ref[0, g], wp_ref[1, g]))
    mix = jnp.concatenate(mixed, axis=-1) * ps_ref[...]
    x1 = _gated_add(x, g1_ref[...], mix)
    x1_ref[...] = x1

    if n_tiles > 1:
        @pl.when(i < n_tiles - 1)
        def _():
            for k in range(POOL_BUF):
                hb_ref[k * batch:(k + 1) * batch, :] = hb_ref[rows + k * batch:rows + (k + 1) * batch, :]

    @pl.when(i == n_tiles - 1)
    def _():
        nbuf_ref[...] = hb_ref[rows:rows + halo, :]

    h2 = _norm_mod(x1, n2_ref[...], sh2_ref[...], sc2_ref[...])
    bucket, rank, wa, wb = _route(h2, rw_ref, rb_ref, run_ref)
    _emit_routed(h2, bucket, rank, wa, wb, h2_ref, route_ref, cnt_ref, run_ref)


def _mod_spec(layer, k, group):
    return pl.BlockSpec((None, None, MOD_ROWS, D), lambda i: (layer, k, group, 0))


def _norm_spec(layer, k):
    return pl.BlockSpec((None, None, 1, D), lambda i: (layer, k, 0, 0))


def _full_spec(shape):
    nd = len(shape)
    return pl.BlockSpec(shape, lambda i: (0,) * nd)


def _const_spec(shape):
    nd = len(shape)
    return pl.BlockSpec(shape, lambda i: (0,) * nd, pipeline_mode=pl.Buffered(1))


def _mixer_out(rows_total, n_tiles, rows):
    out_shape = [
        jax.ShapeDtypeStruct((rows_total, D), jnp.float32),
        jax.ShapeDtypeStruct((rows_total, 1, DX), jnp.float32),
        jax.ShapeDtypeStruct((n_tiles, 8, rows), jnp.float32),
        jax.ShapeDtypeStruct((BUCKET_ROWS, 128), jnp.float32),
    ]
    out_specs = [
        pl.BlockSpec((rows, D), lambda i: (i, 0)),
        pl.BlockSpec((rows, 1, DX), lambda i: (i, 0, 0)),
        pl.BlockSpec((None, 8, rows), lambda i: (i, 0, 0)),
        _full_spec((BUCKET_ROWS, 128)),
    ]
    return out_shape, out_specs


def _pool_layer(x, mod, norm_g4, buf, wp, ps, rw_t, rb, *, layer, group, batch, start):
    rows_total = x.shape[0]
    rows = min(TM_MIX, rows_total)
    steps = rows // batch
    n_tiles = rows_total // rows
    halo = POOL_BUF * batch
    out_shape, out_specs = _mixer_out(rows_total, n_tiles, rows)
    out_shape.append(jax.ShapeDtypeStruct((halo, D), jnp.float32))
    out_specs.append(_full_spec((halo, D)))
    kern = functools.partial(_pool_kernel, batch=batch, steps=steps, n_tiles=n_tiles, start=start)
    return pl.pallas_call(
        kern,
        out_shape=out_shape,
        grid=(n_tiles,),
        in_specs=[
            pl.BlockSpec((rows, D), lambda i: (i, 0)),
            _mod_spec(layer, 0, group), _mod_spec(layer, 1, group), _mod_spec(layer, 2, group),
            _mod_spec(layer, 3, group), _mod_spec(layer, 4, group),
            _norm_spec(layer, 0), _norm_spec(layer, 1),
            _full_spec((halo, D)),
            _full_spec((2, LRU_BLOCKS, POOL_GW, POOL_GW)),
            _full_spec((1, D)),
            _full_spec((N_EXPERTS, D)),
            _full_spec((N_EXPERTS, 1)),
        ],
        out_specs=out_specs,
        scratch_shapes=[
            pltpu.VMEM((halo + rows, D), jnp.float32),
            pltpu.VMEM((BUCKET_ROWS, 128), jnp.float32),
        ],
        compiler_params=pltpu.CompilerParams(
            dimension_semantics=("arbitrary",), vmem_limit_bytes=VMEM_LIMIT),
        name=f"pool_layer_b{batch}",
    )(x, mod, mod, mod, mod, mod, norm_g4, norm_g4, buf, wp, ps, rw_t, rb)


def _lru_kernel(x_ref, sh1_ref, sc1_ref, g1_ref, sh2_ref, sc2_ref, n1_ref, n2_ref,
                cbuf_ref, h0_ref, wx_ref, bx_ref, wy_ref, by_ref, cw_ref, cb_ref,
                wa_ref, ba_ref, wi_ref, bi_ref, lam_ref, wo_ref, bo_ref, rw_ref, rb_ref,
                x1_ref, h2_ref, route_ref, cnt_ref, nconv_ref, nh_ref,
                xp_ref, a_ref, b_ref, hs_ref, hst_ref, run_ref, *, batch, steps, n_tiles, start):
    i = pl.program_id(0)
    rows = steps * batch
    halo = (CONV_WIDTH - 1) * batch

    @pl.when(i == 0)
    def _():
        xp_ref[0:halo, :] = cbuf_ref[...]
        hst_ref[...] = h0_ref[...]
        run_ref[...] = jnp.zeros_like(run_ref)

    x = x_ref[...]
    h_hi, h_lo = _split(_norm_mod(x, n1_ref[...], sh1_ref[...], sc1_ref[...]))
    xb = _dot3(h_hi, h_lo, wx_ref[0], wx_ref[1]) + bx_ref[...]
    yb = jax.nn.gelu(_dot3(h_hi, h_lo, wy_ref[0], wy_ref[1]) + by_ref[...], approximate=True)
    xp_ref[halo:halo + rows, :] = xb
    xc = xp_ref[0:rows, :] * cw_ref[0:1, :]
    for k in range(1, CONV_WIDTH):
        xc = xc + xp_ref[k * batch:k * batch + rows, :] * cw_ref[k:k + 1, :]
    xc = xc + cb_ref[...]
    xc_hi, xc_lo = _split(xc)
    ra, ri = [], []
    for n in range(LRU_BLOCKS):
        cols = slice(n * LRU_BW, (n + 1) * LRU_BW)
        ra.append(_dot3(xc_hi[:, cols], xc_lo[:, cols], wa_ref[0, n], wa_ref[1, n]))
        ri.append(_dot3(xc_hi[:, cols], xc_lo[:, cols], wi_ref[0, n], wi_ref[1, n]))
    r = jax.nn.sigmoid(jnp.concatenate(ra, axis=-1) + ba_ref[...])
    gi = jax.nn.sigmoid(jnp.concatenate(ri, axis=-1) + bi_ref[...])
    neg_lam = -lam_ref[...]
    softplus = jnp.maximum(neg_lam, 0.0) + jnp.log1p(jnp.exp(-jnp.abs(neg_lam)))
    log_a = -LRU_C * r * softplus
    a = jnp.exp(log_a)
    mult = jnp.sqrt(jnp.maximum(1.0 - jnp.exp(2.0 * log_a), 0.0))
    pos = _time_index(i, rows, batch, steps, start)
    mult = jnp.where(pos == 0, 1.0, mult)
    a_ref[...] = a
    b_ref[...] = xc * gi * mult

    hcur = hst_ref[...]
    for t in range(steps):
        sl = slice(t * batch, (t + 1) * batch)
        hcur = a_ref[sl, :] * hcur + b_ref[sl, :]
        hs_ref[sl, :] = hcur
    hst_ref[...] = hcur

    gated_hi, gated_lo = _split(hs_ref[...] * yb)
    y = _dot3(gated_hi, gated_lo, wo_ref[0], wo_ref[1]) + bo_ref[...]
    x1 = _gated_add(x, g1_ref[...], y)
    x1_ref[...] = x1

    if n_tiles > 1:
        @pl.when(i < n_tiles - 1)
        def _():
            for k in range(CONV_WIDTH - 1):
                xp_ref[k * batch:(k + 1) * batch, :] = xp_ref[rows + k * batch:rows + (k + 1) * batch, :]

    @pl.when(i == n_tiles - 1)
    def _():
        nconv_ref[...] = xp_ref[rows:rows + halo, :]
        nh_ref[...] = hcur

    h2 = _norm_mod(x1, n2_ref[...], sh2_ref[...], sc2_ref[...])
    bucket, rank, wa, wb = _route(h2, rw_ref, rb_ref, run_ref)
    _emit_routed(h2, bucket, rank, wa, wb, h2_ref, route_ref, cnt_ref, run_ref)


def _lru_layer(x, mod, norm_g4, cbuf, h0, p, rw_t, rb, *, layer, group, batch, start):
    rows_total = x.shape[0]
    rows = min(TM_MIX, rows_total)
    steps = rows // batch
    n_tiles = rows_total // rows
    halo = (CONV_WIDTH - 1) * batch
    out_shape, out_specs = _mixer_out(rows_total, n_tiles, rows)
    out_shape += [jax.ShapeDtypeStruct((halo, D), jnp.float32),
                  jax.ShapeDtypeStruct((batch, D), jnp.float32)]
    out_specs += [_full_spec((halo, D)), _full_spec((batch, D))]
    kern = functools.partial(_lru_kernel, batch=batch, steps=steps, n_tiles=n_tiles, start=start)
    vec = _full_spec((1, D))
    sq = _const_spec((2, D, D))
    blk = _const_spec((2, LRU_BLOCKS, LRU_BW, LRU_BW))
    return pl.pallas_call(
        kern,
        out_shape=out_shape,
        grid=(n_tiles,),
        in_specs=[
            pl.BlockSpec((rows, D), lambda i: (i, 0)),
            _mod_spec(layer, 0, group), _mod_spec(layer, 1, group), _mod_spec(layer, 2, group),
            _mod_spec(layer, 3, group), _mod_spec(layer, 4, group),
            _norm_spec(layer, 0), _norm_spec(layer, 1),
            _full_spec((halo, D)), _full_spec((batch, D)),
            sq, vec, sq, vec, _full_spec((CONV_WIDTH, D)), vec,
            blk, vec, blk, vec, vec, sq, vec,
            _full_spec((N_EXPERTS, D)), _full_spec((N_EXPERTS, 1)),
        ],
        out_specs=out_specs,
        scratch_shapes=[
            pltpu.VMEM((halo + rows, D), jnp.float32),
            pltpu.VMEM((rows, D), jnp.float32),
            pltpu.VMEM((rows, D), jnp.float32),
            pltpu.VMEM((rows, D), jnp.float32),
            pltpu.VMEM((batch, D), jnp.float32),
            pltpu.VMEM((BUCKET_ROWS, 128), jnp.float32),
        ],
        compiler_params=pltpu.CompilerParams(
            dimension_semantics=("arbitrary",), vmem_limit_bytes=VMEM_LIMIT),
        name=f"lru_layer_b{batch}",
    )(x, mod, mod, mod, mod, mod, norm_g4, norm_g4, cbuf, h0,
      p["w_x"], p["b_x"], p["w_y"], p["b_y"], p["cw"], p["cb"],
      p["w_a"], p["b_a"], p["w_i"], p["b_i"], p["lam"], p["w_out"], p["b_out"], rw_t, rb)


def _row_copy_wait(src_ref, dst_ref, sem):
    pltpu.make_async_copy(src_ref, dst_ref, sem).wait()


def _dispatch_kernel(pos_ref, zt_ref, hp_ref, hs_ref, xs_ref, zero_ref, sem, zsem, *,
                     n_prompt_tiles, n_sorted_tiles):
    i = pl.program_id(0)

    @pl.when(i == 0)
    def _():
        zero_ref[...] = jnp.zeros_like(zero_ref)

        def fill(t):
            return pltpu.make_async_copy(zero_ref, xs_ref.at[pl.ds(t * TMM, TMM)], zsem)

        def start(t, carry):
            @pl.when(zt_ref[t] != 0)
            def _():
                fill(t).start()
            return carry

        def wait(t, carry):
            @pl.when(zt_ref[t] != 0)
            def _():
                fill(t).wait()
            return carry

        lax.fori_loop(0, n_sorted_tiles, start, 0)
        lax.fori_loop(0, n_sorted_tiles, wait, 0)

    def scatter(src_ref):
        def body(c, carry):
            for u in range(ROW_UNROLL):
                r = c * ROW_UNROLL + u
                p = pos_ref[i * TM + r]
                pltpu.make_async_copy(src_ref.at[r], xs_ref.at[p], sem).start(priority=u % 2)
            return carry
        lax.fori_loop(0, TM // ROW_UNROLL, body, 0)
        _row_copy_wait(src_ref, xs_ref.at[pl.ds(0, TM)], sem)

    @pl.when(i < n_prompt_tiles)
    def _():
        scatter(hp_ref)

    @pl.when(i >= n_prompt_tiles)
    def _():
        scatter(hs_ref)


def _dispatch(pos, zero_tiles, h2_p, h2_s):
    n_p = h2_p.shape[0] // TM
    n_s = h2_s.shape[0] // TM
    n_sorted_tiles = zero_tiles.shape[0]
    grid_spec = pltpu.PrefetchScalarGridSpec(
        num_scalar_prefetch=2,
        grid=(n_p + n_s,),
        in_specs=[
            pl.BlockSpec((TM, 1, DX), lambda i, pos, zt: (jnp.minimum(i, n_p - 1), 0, 0)),
            pl.BlockSpec((TM, 1, DX), lambda i, pos, zt: (jnp.maximum(i - n_p, 0), 0, 0)),
        ],
        out_specs=pl.BlockSpec(memory_space=pl.ANY),
        scratch_shapes=[pltpu.VMEM((TMM, 1, DX), jnp.float32),
                        pltpu.SemaphoreType.DMA, pltpu.SemaphoreType.DMA],
    )
    return pl.pallas_call(
        functools.partial(_dispatch_kernel, n_prompt_tiles=n_p, n_sorted_tiles=n_sorted_tiles),
        out_shape=jax.ShapeDtypeStruct((n_sorted_tiles * TMM, 1, DX), jnp.float32),
        grid_spec=grid_spec,
        compiler_params=pltpu.CompilerParams(
            dimension_semantics=("arbitrary",), vmem_limit_bytes=VMEM_LIMIT),
        name="dispatch",
    )(pos, zero_tiles, h2_p, h2_s)


def _split_into(w_ref, s_ref):
    n_rows = w_ref.shape[0]
    for c in range(0, n_rows, SPLIT_ROWS):
        hi, lo = _split(w_ref[c:c + SPLIT_ROWS, :])
        s_ref[0, c:c + SPLIT_ROWS, :] = hi
        s_ref[1, c:c + SPLIT_ROWS, :] = lo


def _expert_kernel(ea_ref, eb_ref, nu_ref, xs_ref, fga_ref, fua_ref, fda_ref, fgb_ref, fub_ref,
                   fdb_ref, ys_ref, x2d_ref, wga_ref, wua_ref, wda_ref, wgb_ref, wub_ref, wdb_ref):
    j = pl.program_id(0)
    prev = jnp.maximum(j - 1, 0)

    @pl.when((j == 0) | (ea_ref[j] != ea_ref[prev]))
    def _():
        _split_into(fga_ref, wga_ref)
        _split_into(fua_ref, wua_ref)
        _split_into(fda_ref, wda_ref)

    @pl.when((j == 0) | (eb_ref[j] != eb_ref[prev]))
    def _():
        _split_into(fgb_ref, wgb_ref)
        _split_into(fub_ref, wub_ref)
        _split_into(fdb_ref, wdb_ref)

    @pl.when(j < nu_ref[0])
    def _():
        x2d_ref[...] = xs_ref[...].reshape(TMM, DX)
        x_hi, x_lo = _split(x2d_ref[:, 0:D])
        gate_a = x2d_ref[:, D:D + 1]
        gate_b = x2d_ref[:, D + 1:D + 2]

        def expert(wg_ref, wu_ref, wd_ref, gate):
            hg = _dot3(x_hi, x_lo, wg_ref[0], wg_ref[1])
            hu = _dot3(x_hi, x_lo, wu_ref[0], wu_ref[1])
            hid_hi, hid_lo = _split(hg * jax.nn.sigmoid(hg) * hu * gate)
            return _dot3(hid_hi, hid_lo, wd_ref[0], wd_ref[1])

        y = expert(wga_ref, wua_ref, wda_ref, gate_a) + expert(wgb_ref, wub_ref, wdb_ref, gate_b)
        ys_ref[...] = y.reshape(TMM, 1, D)

    @pl.when(j >= nu_ref[0])
    def _():
        ys_ref[...] = jnp.zeros_like(ys_ref)


def _experts(ea, eb, n_used, xs, wg, wu, wd, *, layer):
    n_tiles = xs.shape[0] // TMM

    def x_map(j, ea, eb, nu):
        return (jnp.minimum(j, nu[0] - 1), 0, 0)

    def a_map(j, ea, eb, nu):
        return (layer, ea[j], 0, 0)

    def b_map(j, ea, eb, nu):
        return (layer, eb[j], 0, 0)

    up = (None, None, D, D_FF)
    down = (None, None, D_FF, D)
    up_split = pltpu.VMEM((2, D, D_FF), jnp.bfloat16)
    down_split = pltpu.VMEM((2, D_FF, D), jnp.bfloat16)
    grid_spec = pltpu.PrefetchScalarGridSpec(
        num_scalar_prefetch=3,
        grid=(n_tiles,),
        in_specs=[
            pl.BlockSpec((TMM, 1, DX), x_map),
            pl.BlockSpec(up, a_map), pl.BlockSpec(up, a_map), pl.BlockSpec(down, a_map),
            pl.BlockSpec(up, b_map), pl.BlockSpec(up, b_map), pl.BlockSpec(down, b_map),
        ],
        out_specs=pl.BlockSpec((TMM, 1, D), lambda j, ea, eb, nu: (j, 0, 0)),
        scratch_shapes=[pltpu.VMEM((TMM, DX), jnp.float32),
                        up_split, up_split, down_split, up_split, up_split, down_split],
    )
    return pl.pallas_call(
        _expert_kernel,
        out_shape=jax.ShapeDtypeStruct((xs.shape[0], 1, D), jnp.float32),
        grid_spec=grid_spec,
        compiler_params=pltpu.CompilerParams(
            dimension_semantics=("arbitrary",), vmem_limit_bytes=VMEM_LIMIT),
        name="experts",
    )(ea, eb, n_used, xs, wg, wu, wd, wg, wu, wd)


def _combine_kernel(pos_ref, xp_ref, xs_ref, gp_ref, gs_ref, fg_ref, ys_ref, op_ref, os_ref,
                    g3_ref, g2_ref, sem, *, n_prompt_tiles, final):
    i = pl.program_id(0)

    def body(c, carry):
        for u in range(ROW_UNROLL):
            r = c * ROW_UNROLL + u
            p = pos_ref[i * TM + r]
            pltpu.make_async_copy(ys_ref.at[p], g3_ref.at[r], sem).start(priority=u % 2)
        return carry
    lax.fori_loop(0, TM // ROW_UNROLL, body, 0)
    _row_copy_wait(ys_ref.at[pl.ds(0, TM)], g3_ref, sem)
    g2_ref[...] = g3_ref[...].reshape(TM, D)

    def finish(x_ref, gate_ref, o_ref):
        x2 = _gated_add(x_ref[...], gate_ref[...], g2_ref[...])
        if final:
            ms = jnp.mean(x2 * x2, axis=-1, keepdims=True)
            x2 = x2 * lax.rsqrt(ms + EPS) * fg_ref[...]
        o_ref[...] = x2

    @pl.when(i < n_prompt_tiles)
    def _():
        finish(xp_ref, gp_ref, op_ref)

    @pl.when(i >= n_prompt_tiles)
    def _():
        finish(xs_ref, gs_ref, os_ref)


def _combine(pos, x1_p, x1_s, mod, final_g, ys, *, layer, final):
    n_p = x1_p.shape[0] // TM
    n_s = x1_s.shape[0] // TM

    def p_map(i, pos):
        return (jnp.minimum(i, n_p - 1), 0)

    def s_map(i, pos):
        return (jnp.maximum(i - n_p, 0), 0)

    grid_spec = pltpu.PrefetchScalarGridSpec(
        num_scalar_prefetch=1,
        grid=(n_p + n_s,),
        in_specs=[
            pl.BlockSpec((TM, D), p_map),
            pl.BlockSpec((TM, D), s_map),
            pl.BlockSpec((None, None, MOD_ROWS, D), lambda i, pos: (layer, 5, 0, 0)),
            pl.BlockSpec((None, None, MOD_ROWS, D), lambda i, pos: (layer, 5, 1, 0)),
            pl.BlockSpec((1, D), lambda i, pos: (0, 0)),
            pl.BlockSpec(memory_space=pl.ANY),
        ],
        out_specs=[pl.BlockSpec((TM, D), p_map), pl.BlockSpec((TM, D), s_map)],
        scratch_shapes=[
            pltpu.VMEM((TM, 1, D), jnp.float32),
            pltpu.VMEM((TM, D), jnp.float32),
            pltpu.SemaphoreType.DMA,
        ],
    )
    return pl.pallas_call(
        functools.partial(_combine_kernel, n_prompt_tiles=n_p, final=final),
        out_shape=[jax.ShapeDtypeStruct(x1_p.shape, jnp.float32),
                   jax.ShapeDtypeStruct(x1_s.shape, jnp.float32)],
        grid_spec=grid_spec,
        compiler_params=pltpu.CompilerParams(
            dimension_semantics=("arbitrary",), vmem_limit_bytes=VMEM_LIMIT),
        name="combine",
    )(pos, x1_p, x1_s, mod, mod, final_g, ys)


def _plan(route_p, route_s, cnt_p, cnt_s, n_sorted_tiles):
    b_p = route_p[:, 0, :].reshape(-1).astype(jnp.int32)
    r_p = route_p[:, 1, :].reshape(-1).astype(jnp.int32)
    b_s = route_s[:, 0, :].reshape(-1).astype(jnp.int32)
    r_s = route_s[:, 1, :].reshape(-1).astype(jnp.int32)
    c_p = cnt_p[:, 0].astype(jnp.int32)
    c_s = cnt_s[:, 0].astype(jnp.int32)
    tiles_per_bucket = (c_p + c_s + TMM - 1) // TMM
    tile_end = jnp.cumsum(tiles_per_bucket)
    row_start = (tile_end - tiles_per_bucket) * TMM
    pos = jnp.concatenate([row_start[b_p] + r_p, row_start[b_s] + c_p[b_s] + r_s])
    n_used = tile_end[-1]
    tile = jnp.minimum(jnp.arange(n_sorted_tiles, dtype=jnp.int32), n_used - 1)
    tile_bucket = jnp.sum((tile[:, None] >= tile_end[None, :]).astype(jnp.int32), axis=1)
    group = tile_bucket // 6
    pair = tile_bucket % 6
    ea = GROUP_SIZE * group + jnp.asarray(_PAIR_A, jnp.int32)[pair]
    eb = GROUP_SIZE * group + jnp.asarray(_PAIR_B, jnp.int32)[pair]
    all_tiles = jnp.arange(n_sorted_tiles, dtype=jnp.int32)
    last_of_bucket = jnp.any((all_tiles[:, None] == tile_end[None, :] - 1)
                             & (tiles_per_bucket[None, :] > 0), axis=1)
    zero_tiles = (last_of_bucket | (all_tiles >= n_used)).astype(jnp.int32)
    return pos, ea, eb, n_used.reshape(1), zero_tiles


def _time_major(x):
    b, s, d = x.shape
    return jnp.transpose(x, (1, 0, 2)).reshape(s * b, d)


def _batch_major(x, batch):
    rows, d = x.shape
    return jnp.transpose(x.reshape(rows // batch, batch, d), (1, 0, 2))


def kernel(x_prompt, x_sample, c_prompt, c_sample, state_pool_buf, state_conv_buf, state_lru_h,
           w_mod, b_mod, norm_g, pool_w, pool_scale, lru_w_x, lru_b_x, lru_w_y, lru_b_y,
           conv_w, conv_b, lru_w_a, lru_b_a, lru_w_i, lru_b_i, lru_lambda, lru_w_out, lru_b_out,
           router_w, router_b, moe_w_gate, moe_w_up, moe_w_down, final_g):
    bp, sp, _ = x_prompt.shape
    bs, ss, _ = x_sample.shape
    f32 = jnp.float32

    c_exp = jnp.concatenate([jnp.tile(c_prompt, (MOD_ROWS // bp, 1)),
                             jnp.tile(c_sample, (MOD_ROWS // bs, 1))], axis=0)
    mod = _modulation(c_exp, w_mod, b_mod)
    norm_g4 = norm_g.reshape(DEPTH, 2, 1, D)
    rw_t = router_w.T
    rb = router_b.reshape(N_EXPERTS, 1)
    fg = final_g.reshape(1, D)

    xp = _time_major(x_prompt)
    xs = _time_major(x_sample)
    n_tokens = xp.shape[0] + xs.shape[0]
    n_sorted_tiles = n_tokens // TMM + N_BUCKETS

    groups = (
        dict(batch=bp, start=0, group=0),
        dict(batch=bs, start=PAST_LEN, group=1),
    )
    new_pool = ([], [])
    new_conv = ([], [])
    new_h = ([], [])
    x = [xp, xs]
    for layer in range(DEPTH):
        j = layer // 2
        outs = []
        for gi, g in enumerate(groups):
            b = g["batch"]
            if layer % 2 == 0:
                if gi == 0:
                    buf = jnp.zeros((POOL_BUF * b, D), f32)
                else:
                    buf = _time_major(state_pool_buf[j])
                x1, h2, route, cnt, nbuf = _pool_layer(
                    x[gi], mod, norm_g4, buf, _split_weight(pool_w[j]), pool_scale[j].reshape(1, D),
                    rw_t, rb, layer=layer, **g)
                new_pool[gi].append(_batch_major(nbuf, b))
            else:
                if gi == 0:
                    cbuf = jnp.zeros(((CONV_WIDTH - 1) * b, D), f32)
                    h0 = jnp.zeros((b, D), f32)
                else:
                    cbuf = _time_major(state_conv_buf[j])
                    h0 = state_lru_h[j]
                p = dict(
                    w_x=_split_weight(lru_w_x[j]), b_x=lru_b_x[j].reshape(1, D),
                    w_y=_split_weight(lru_w_y[j]), b_y=lru_b_y[j].reshape(1, D),
                    cw=conv_w[j], cb=conv_b[j].reshape(1, D),
                    w_a=_split_weight(lru_w_a[j]), b_a=lru_b_a[j].reshape(1, D),
                    w_i=_split_weight(lru_w_i[j]), b_i=lru_b_i[j].reshape(1, D),
                    lam=lru_lambda[j].reshape(1, D),
                    w_out=_split_weight(lru_w_out[j]), b_out=lru_b_out[j].reshape(1, D))
                x1, h2, route, cnt, nconv, nh = _lru_layer(
                    x[gi], mod, norm_g4, cbuf, h0, p, rw_t, rb, layer=layer, **g)
                new_conv[gi].append(_batch_major(nconv, b))
                new_h[gi].append(nh)
            outs.append((x1, h2, route, cnt))
        (x1_p, h2_p, route_p, cnt_p), (x1_s, h2_s, route_s, cnt_s) = outs
        pos, ea, eb, n_used, zero_tiles = _plan(route_p, route_s, cnt_p, cnt_s, n_sorted_tiles)
        xsorted = _dispatch(pos, zero_tiles, h2_p, h2_s)
        ysorted = _experts(ea, eb, n_used, xsorted, moe_w_gate, moe_w_up, moe_w_down, layer=layer)
        x = list(_combine(pos, x1_p, x1_s, mod, fg, ysorted, layer=layer,
                          final=(layer == DEPTH - 1)))

    y_prompt = _batch_major(x[0], bp)
    y_sample = _batch_major(x[1], bs)
    return (y_prompt, y_sample,
            jnp.stack(new_pool[0]), jnp.stack(new_pool[1]),
            jnp.stack(new_conv[0]), jnp.stack(new_conv[1]),
            jnp.stack(new_h[0]), jnp.stack(new_h[1]))
```

```python
import functools

import jax
import jax.numpy as jnp
from jax import lax
from jax.experimental import pallas as pl
from jax.experimental.pallas import tpu as pltpu

D = 1024
DEPTH = 4
PAST_LEN = 16384
POOL_WINDOWS = (2, 4, 8, 16)
POOL_GW = 256
POOL_BUF = 15
LRU_BLOCKS = 4
LRU_BW = 256
CONV_WIDTH = 4
LRU_C = 8.0
N_EXPERTS = 16
GROUP_SIZE = 4
D_FF = 512
EPS = 1e-6

MOD_ROWS = 128
TM = 512
TM_MIX = 512
TMM = 256
SPLIT_ROWS = 128
ROW_UNROLL = 8
GATE_LANES = 128
DX = D + GATE_LANES
N_BUCKETS = 24
BUCKET_ROWS = 32
VMEM_LIMIT = 56 * 1024 * 1024

_PAIR_A = (0, 0, 0, 1, 1, 2)
_PAIR_B = (1, 2, 3, 2, 3, 3)


def _bf(x):
    return x.astype(jnp.bfloat16)


def _split(x):
    hi = _bf(x)
    lo = _bf(x - hi.astype(jnp.float32))
    return hi, lo


def _split_weight(w):
    bits = lax.bitcast_convert_type(w, jnp.uint32) & jnp.uint32(0xFFFF0000)
    hi = lax.bitcast_convert_type(bits, jnp.float32)
    return jnp.stack([_bf(hi), _bf(w - hi)], axis=0)


def _dot(a, b):
    return jnp.dot(a, b, preferred_element_type=jnp.float32)


def _dot3(a_hi, a_lo, w_hi, w_lo):
    rows = a_hi.shape[0]
    both = _dot(jnp.concatenate([a_hi, a_lo], axis=0), w_hi)
    return both[0:rows] + both[rows:2 * rows] + _dot(a_hi, w_lo)


def _dot_nt(a, b):
    return lax.dot_general(a, b, (((1,), (1,)), ((), ())), preferred_element_type=jnp.float32)


def _mod_kernel(c_ref, w_ref, b_ref, o_ref):
    c = c_ref[...]
    cs = c * jax.nn.sigmoid(c)
    c_hi, c_lo = _split(cs)
    w_hi, w_lo = _split(w_ref[...])
    o_ref[...] = _dot(c_hi, w_hi) + _dot(c_lo, w_hi) + _dot(c_hi, w_lo) + b_ref[...]


def _modulation(c_exp, w_mod, b_mod):
    rows = c_exp.shape[0]
    return pl.pallas_call(
        _mod_kernel,
        out_shape=jax.ShapeDtypeStruct((DEPTH, 6, rows, D), jnp.float32),
        grid=(DEPTH, 6),
        in_specs=[
            pl.BlockSpec((rows, D), lambda i, n: (0, 0)),
            pl.BlockSpec((None, D, D), lambda i, n: (i, 0, n)),
            pl.BlockSpec((None, None, 1, D), lambda i, n: (i, n, 0, 0)),
        ],
        out_specs=pl.BlockSpec((None, None, rows, D), lambda i, n: (i, n, 0, 0)),
        compiler_params=pltpu.CompilerParams(
            dimension_semantics=("arbitrary", "arbitrary"), vmem_limit_bytes=VMEM_LIMIT),
        name="modulation",
    )(c_exp, w_mod, b_mod.reshape(DEPTH, 6, 1, D))


def _norm_mod(x, gamma, shift, scale):
    rows = x.shape[0]
    ms = jnp.mean(x * x, axis=-1, keepdims=True)
    y = x * lax.rsqrt(ms + EPS) * gamma
    y3 = y.reshape(rows // MOD_ROWS, MOD_ROWS, D)
    return (y3 * (1.0 + scale)[None] + shift[None]).reshape(rows, D)


def _gated_add(x, gate, y):
    rows = x.shape[0]
    y3 = y.reshape(rows // MOD_ROWS, MOD_ROWS, D)
    return x + (y3 * gate[None]).reshape(rows, D)


def _route(h2, rw_ref, rb_ref, run_ref):
    rows = h2.shape[0]
    h_hi, h_lo = _split(h2)
    w_hi, w_lo = _split(rw_ref[...])
    logits = _dot_nt(w_hi, h_hi) + _dot_nt(w_lo, h_hi) + _dot_nt(w_hi, h_lo)
    m = jnp.max(logits, axis=0, keepdims=True)
    e = jnp.exp(logits - m)
    scores = e / jnp.sum(e, axis=0, keepdims=True)
    sel = scores + rb_ref[...]
    eidx = lax.broadcasted_iota(jnp.int32, (N_EXPERTS, rows), 0).astype(jnp.float32)
    gidx = jnp.floor(eidx * (1.0 / GROUP_SIZE))
    neg = -jnp.inf
    sentinel = float(N_EXPERTS)

    def top2(vals):
        v1 = jnp.max(vals, axis=0, keepdims=True)
        i1 = jnp.min(jnp.where(vals == v1, eidx, sentinel), axis=0, keepdims=True)
        rest = jnp.where(eidx == i1, neg, vals)
        v2 = jnp.max(rest, axis=0, keepdims=True)
        i2 = jnp.min(jnp.where(rest == v2, eidx, sentinel), axis=0, keepdims=True)
        return v1, i1, v2, i2

    best = jnp.zeros((1, rows), jnp.float32)
    best_v = None
    for g in range(N_EXPERTS // GROUP_SIZE):
        v1, _, v2, _ = top2(jnp.where(gidx == float(g), sel, neg))
        gs = v1 + v2
        if best_v is None:
            best_v = gs
        else:
            upd = gs > best_v
            best = jnp.where(upd, float(g), best)
            best_v = jnp.where(upd, gs, best_v)
    _, i1, _, i2 = top2(jnp.where(gidx == best, sel, neg))
    chosen = (eidx == i1) | (eidx == i2)
    ssum = jnp.sum(jnp.where(chosen, scores, 0.0), axis=0, keepdims=True)
    gates = jnp.where(chosen, scores / ssum, 0.0)
    ea = jnp.minimum(i1, i2)
    eb = jnp.maximum(i1, i2)
    wa = jnp.sum(jnp.where(eidx == ea, gates, 0.0), axis=0, keepdims=True)
    wb = jnp.sum(jnp.where(eidx == eb, gates, 0.0), axis=0, keepdims=True)
    a = ea - GROUP_SIZE * best
    b = eb - GROUP_SIZE * best
    pair = jnp.where(a == 0.0, b - 1.0, jnp.where(a == 1.0, b + 1.0, 5.0))
    bucket = 6.0 * best + pair

    bidx = lax.broadcasted_iota(jnp.int32, (BUCKET_ROWS, rows), 0).astype(jnp.float32)
    onehot = jnp.where(bidx == bucket, 1.0, 0.0)
    src = lax.broadcasted_iota(jnp.int32, (rows, rows), 0)
    dst = lax.broadcasted_iota(jnp.int32, (rows, rows), 1)
    before = _bf(jnp.where(src < dst, 1.0, 0.0))
    earlier = _dot(_bf(onehot), before)
    base = run_ref[:, 0:1]
    rank = jnp.sum(onehot * (earlier + base), axis=0, keepdims=True)
    run_ref[...] = run_ref[...] + jnp.sum(onehot, axis=1, keepdims=True)
    return bucket, rank, wa, wb


def _emit_routed(h2, bucket, rank, wa, wb, h2_ref, route_ref, cnt_ref, run_ref):
    rows = h2.shape[0]
    lane_row = lax.broadcasted_iota(jnp.int32, (GATE_LANES, rows), 0)
    gate_t = jnp.where(lane_row == 0, wa, jnp.where(lane_row == 1, wb, 0.0))
    h2_ref[:, :, 0:D] = h2.reshape(rows, 1, D)
    h2_ref[:, :, D:DX] = gate_t.T.reshape(rows, 1, GATE_LANES)
    info_row = lax.broadcasted_iota(jnp.int32, (8, rows), 0)
    route_ref[...] = jnp.where(info_row == 0, bucket, jnp.where(info_row == 1, rank, 0.0))
    cnt_ref[...] = run_ref[...]


def _time_index(i, rows, batch, steps, start):
    r = lax.broadcasted_iota(jnp.int32, (rows, 1), 0)
    return start + i * steps + r // batch


def _pool_kernel(x_ref, sh1_ref, sc1_ref, g1_ref, sh2_ref, sc2_ref, n1_ref, n2_ref, buf_ref,
                 wp_ref, ps_ref, rw_ref, rb_ref,
                 x1_ref, h2_ref, route_ref, cnt_ref, nbuf_ref,
                 hb_ref, run_ref, *, batch, steps, n_tiles, start):
    i = pl.program_id(0)
    rows = steps * batch
    halo = POOL_BUF * batch

    @pl.when(i == 0)
    def _():
        hb_ref[0:halo, :] = buf_ref[...]
        run_ref[...] = jnp.zeros_like(run_ref)

    x = x_ref[...]
    h = _norm_mod(x, n1_ref[...], sh1_ref[...], sc1_ref[...])
    hb_ref[halo:halo + rows, :] = h
    pos = _time_index(i, rows, batch, steps, start)
    mixed = []
    for g, w in enumerate(POOL_WINDOWS):
        c0, c1 = g * POOL_GW, (g + 1) * POOL_GW
        acc = hb_ref[halo:halo + rows, c0:c1]
        for k in range(1, w):
            off = (POOL_BUF - k) * batch
            acc = acc + hb_ref[off:off + rows, c0:c1]
        inv_cnt = 1.0 / jnp.minimum(pos + 1, w).astype(jnp.float32)
        pooled = acc * inv_cnt - h[:, c0:c1]
        p_hi, p_lo = _split(pooled)
        mixed.append(_dot3(p_hi, p_lo, wp_ref[0, g], wp_ref[1, g]))
    mix = jnp.concatenate(mixed, axis=-1) * ps_ref[...]
    x1 = _gated_add(x, g1_ref[...], mix)
    x1_ref[...] = x1

    if n_tiles > 1:
        @pl.when(i < n_tiles - 1)
        def _():
            for k in range(POOL_BUF):
                hb_ref[k * batch:(k + 1) * batch, :] = hb_ref[rows + k * batch:rows + (k + 1) * batch, :]

    @pl.when(i == n_tiles - 1)
    def _():
        nbuf_ref[...] = hb_ref[rows:rows + halo, :]

    h2 = _norm_mod(x1, n2_ref[...], sh2_ref[...], sc2_ref[...])
    bucket, rank, wa, wb = _route(h2, rw_ref, rb_ref, run_ref)
    _emit_routed(h2, bucket, rank, wa, wb, h2_ref, route_ref, cnt_ref, run_ref)


def _mod_spec(layer, k, group):
    return pl.BlockSpec((None, None, MOD_ROWS, D), lambda i: (layer, k, group, 0))


def _norm_spec(layer, k):
    return pl.BlockSpec((None, None, 1, D), lambda i: (layer, k, 0, 0))


def _full_spec(shape):
    nd = len(shape)
    return pl.BlockSpec(shape, lambda i: (0,) * nd)


def _const_spec(shape):
    nd = len(shape)
    return pl.BlockSpec(shape, lambda i: (0,) * nd, pipeline_mode=pl.Buffered(1))


def _mixer_out(rows_total, n_tiles, rows):
    out_shape = [
        jax.ShapeDtypeStruct((rows_total, D), jnp.float32),
        jax.ShapeDtypeStruct((rows_total, 1, DX), jnp.float32),
        jax.ShapeDtypeStruct((n_tiles, 8, rows), jnp.float32),
        jax.ShapeDtypeStruct((BUCKET_ROWS, 128), jnp.float32),
    ]
    out_specs = [
        pl.BlockSpec((rows, D), lambda i: (i, 0)),
        pl.BlockSpec((rows, 1, DX), lambda i: (i, 0, 0)),
        pl.BlockSpec((None, 8, rows), lambda i: (i, 0, 0)),
        _full_spec((BUCKET_ROWS, 128)),
    ]
    return out_shape, out_specs


def _pool_layer(x, mod, norm_g4, buf, wp, ps, rw_t, rb, *, layer, group, batch, start):
    rows_total = x.shape[0]
    rows = min(TM_MIX, rows_total)
    steps = rows // batch
    n_tiles = rows_total // rows
    halo = POOL_BUF * batch
    out_shape, out_specs = _mixer_out(rows_total, n_tiles, rows)
    out_shape.append(jax.ShapeDtypeStruct((halo, D), jnp.float32))
    out_specs.append(_full_spec((halo, D)))
    kern = functools.partial(_pool_kernel, batch=batch, steps=steps, n_tiles=n_tiles, start=start)
    return pl.pallas_call(
        kern,
        out_shape=out_shape,
        grid=(n_tiles,),
        in_specs=[
            pl.BlockSpec((rows, D), lambda i: (i, 0)),
            _mod_spec(layer, 0, group), _mod_spec(layer, 1, group), _mod_spec(layer, 2, group),
            _mod_spec(layer, 3, group), _mod_spec(layer, 4, group),
            _norm_spec(layer, 0), _norm_spec(layer, 1),
            _full_spec((halo, D)),
            _full_spec((2, LRU_BLOCKS, POOL_GW, POOL_GW)),
            _full_spec((1, D)),
            _full_spec((N_EXPERTS, D)),
            _full_spec((N_EXPERTS, 1)),
        ],
        out_specs=out_specs,
        scratch_shapes=[
            pltpu.VMEM((halo + rows, D), jnp.float32),
            pltpu.VMEM((BUCKET_ROWS, 128), jnp.float32),
        ],
        compiler_params=pltpu.CompilerParams(
            dimension_semantics=("arbitrary",), vmem_limit_bytes=VMEM_LIMIT),
        name=f"pool_layer_b{batch}",
    )(x, mod, mod, mod, mod, mod, norm_g4, norm_g4, buf, wp, ps, rw_t, rb)


def _lru_kernel(x_ref, sh1_ref, sc1_ref, g1_ref, sh2_ref, sc2_ref, n1_ref, n2_ref,
                cbuf_ref, h0_ref, wx_ref, bx_ref, wy_ref, by_ref, cw_ref, cb_ref,
                wa_ref, ba_ref, wi_ref, bi_ref, lam_ref, wo_ref, bo_ref, rw_ref, rb_ref,
                x1_ref, h2_ref, route_ref, cnt_ref, nconv_ref, nh_ref,
                xp_ref, a_ref, b_ref, hs_ref, hst_ref, run_ref, *, batch, steps, n_tiles, start):
    i = pl.program_id(0)
    rows = steps * batch
    halo = (CONV_WIDTH - 1) * batch

    @pl.when(i == 0)
    def _():
        xp_ref[0:halo, :] = cbuf_ref[...]
        hst_ref[...] = h0_ref[...]
        run_ref[...] = jnp.zeros_like(run_ref)

    x = x_ref[...]
    h_hi, h_lo = _split(_norm_mod(x, n1_ref[...], sh1_ref[...], sc1_ref[...]))
    xb = _dot3(h_hi, h_lo, wx_ref[0], wx_ref[1]) + bx_ref[...]
    yb = jax.nn.gelu(_dot3(h_hi, h_lo, wy_ref[0], wy_ref[1]) + by_ref[...], approximate=True)
    xp_ref[halo:halo + rows, :] = xb
    xc = xp_ref[0:rows, :] * cw_ref[0:1, :]
    for k in range(1, CONV_WIDTH):
        xc = xc + xp_ref[k * batch:k * batch + rows, :] * cw_ref[k:k + 1, :]
    xc = xc + cb_ref[...]
    xc_hi, xc_lo = _split(xc)
    ra, ri = [], []
    for n in range(LRU_BLOCKS):
        cols = slice(n * LRU_BW, (n + 1) * LRU_BW)
        ra.append(_dot3(xc_hi[:, cols], xc_lo[:, cols], wa_ref[0, n], wa_ref[1, n]))
        ri.append(_dot3(xc_hi[:, cols], xc_lo[:, cols], wi_ref[0, n], wi_ref[1, n]))
    r = jax.nn.sigmoid(jnp.concatenate(ra, axis=-1) + ba_ref[...])
    gi = jax.nn.sigmoid(jnp.concatenate(ri, axis=-1) + bi_ref[...])
    neg_lam = -lam_ref[...]
    softplus = jnp.maximum(neg_lam, 0.0) + jnp.log1p(jnp.exp(-jnp.abs(neg_lam)))
    log_a = -LRU_C * r * softplus
    a = jnp.exp(log_a)
    mult = jnp.sqrt(jnp.maximum(1.0 - jnp.exp(2.0 * log_a), 0.0))
    pos = _time_index(i, rows, batch, steps, start)
    mult = jnp.where(pos == 0, 1.0, mult)
    a_ref[...] = a
    b_ref[...] = xc * gi * mult

    hcur = hst_ref[...]
    for t in range(steps):
        sl = slice(t * batch, (t + 1) * batch)
        hcur = a_ref[sl, :] * hcur + b_ref[sl, :]
        hs_ref[sl, :] = hcur
    hst_ref[...] = hcur

    gated_hi, gated_lo = _split(hs_ref[...] * yb)
    y = _dot3(gated_hi, gated_lo, wo_ref[0], wo_ref[1]) + bo_ref[...]
    x1 = _gated_add(x, g1_ref[...], y)
    x1_ref[...] = x1

    if n_tiles > 1:
        @pl.when(i < n_tiles - 1)
        def _():
            for k in range(CONV_WIDTH - 1):
                xp_ref[k * batch:(k + 1) * batch, :] = xp_ref[rows + k * batch:rows + (k + 1) * batch, :]

    @pl.when(i == n_tiles - 1)
    def _():
        nconv_ref[...] = xp_ref[rows:rows + halo, :]
        nh_ref[...] = hcur

    h2 = _norm_mod(x1, n2_ref[...], sh2_ref[...], sc2_ref[...])
    bucket, rank, wa, wb = _route(h2, rw_ref, rb_ref, run_ref)
    _emit_routed(h2, bucket, rank, wa, wb, h2_ref, route_ref, cnt_ref, run_ref)


def _lru_layer(x, mod, norm_g4, cbuf, h0, p, rw_t, rb, *, layer, group, batch, start):
    rows_total = x.shape[0]
    rows = min(TM_MIX, rows_total)
    steps = rows // batch
    n_tiles = rows_total // rows
    halo = (CONV_WIDTH - 1) * batch
    out_shape, out_specs = _mixer_out(rows_total, n_tiles, rows)
    out_shape += [jax.ShapeDtypeStruct((halo, D), jnp.float32),
                  jax.ShapeDtypeStruct((batch, D), jnp.float32)]
    out_specs += [_full_spec((halo, D)), _full_spec((batch, D))]
    kern = functools.partial(_lru_kernel, batch=batch, steps=steps, n_tiles=n_tiles, start=start)
    vec = _full_spec((1, D))
    sq = _const_spec((2, D, D))
    blk = _const_spec((2, LRU_BLOCKS, LRU_BW, LRU_BW))
    return pl.pallas_call(
        kern,
        out_shape=out_shape,
        grid=(n_tiles,),
        in_specs=[
            pl.BlockSpec((rows, D), lambda i: (i, 0)),
            _mod_spec(layer, 0, group), _mod_spec(layer, 1, group), _mod_spec(layer, 2, group),
            _mod_spec(layer, 3, group), _mod_spec(layer, 4, group),
            _norm_spec(layer, 0), _norm_spec(layer, 1),
            _full_spec((halo, D)), _full_spec((batch, D)),
            sq, vec, sq, vec, _full_spec((CONV_WIDTH, D)), vec,
            blk, vec, blk, vec, vec, sq, vec,
            _full_spec((N_EXPERTS, D)), _full_spec((N_EXPERTS, 1)),
        ],
        out_specs=out_specs,
        scratch_shapes=[
            pltpu.VMEM((halo + rows, D), jnp.float32),
            pltpu.VMEM((rows, D), jnp.float32),
            pltpu.VMEM((rows, D), jnp.float32),
            pltpu.VMEM((rows, D), jnp.float32),
            pltpu.VMEM((batch, D), jnp.float32),
            pltpu.VMEM((BUCKET_ROWS, 128), jnp.float32),
        ],
        compiler_params=pltpu.CompilerParams(
            dimension_semantics=("arbitrary",), vmem_limit_bytes=VMEM_LIMIT),
        name=f"lru_layer_b{batch}",
    )(x, mod, mod, mod, mod, mod, norm_g4, norm_g4, cbuf, h0,
      p["w_x"], p["b_x"], p["w_y"], p["b_y"], p["cw"], p["cb"],
      p["w_a"], p["b_a"], p["w_i"], p["b_i"], p["lam"], p["w_out"], p["b_out"], rw_t, rb)


def _row_copy_wait(src_ref, dst_ref, sem):
    pltpu.make_async_copy(src_ref, dst_ref, sem).wait()


def _dispatch_kernel(pos_ref, zt_ref, hp_ref, hs_ref, xs_ref, zero_ref, sem, zsem, *,
                     n_prompt_tiles, n_sorted_tiles):
    i = pl.program_id(0)

    @pl.when(i == 0)
    def _():
        zero_ref[...] = jnp.zeros_like(zero_ref)

        def fill(t):
            return pltpu.make_async_copy(zero_ref, xs_ref.at[pl.ds(t * TMM, TMM)], zsem)

        def start(t, carry):
            @pl.when(zt_ref[t] != 0)
            def _():
                fill(t).start()
            return carry

        def wait(t, carry):
            @pl.when(zt_ref[t] != 0)
            def _():
                fill(t).wait()
            return carry

        lax.fori_loop(0, n_sorted_tiles, start, 0)
        lax.fori_loop(0, n_sorted_tiles, wait, 0)

    def scatter(src_ref):
        def body(c, carry):
            for u in range(ROW_UNROLL):
                r = c * ROW_UNROLL + u
                p = pos_ref[i * TM + r]
                pltpu.make_async_copy(src_ref.at[r], xs_ref.at[p], sem).start(priority=u % 2)
            return carry
        lax.fori_loop(0, TM // ROW_UNROLL, body, 0)
        _row_copy_wait(src_ref, xs_ref.at[pl.ds(0, TM)], sem)

    @pl.when(i < n_prompt_tiles)
    def _():
        scatter(hp_ref)

    @pl.when(i >= n_prompt_tiles)
    def _():
        scatter(hs_ref)


def _dispatch(pos, zero_tiles, h2_p, h2_s):
    n_p = h2_p.shape[0] // TM
    n_s = h2_s.shape[0] // TM
    n_sorted_tiles = zero_tiles.shape[0]
    grid_spec = pltpu.PrefetchScalarGridSpec(
        num_scalar_prefetch=2,
        grid=(n_p + n_s,),
        in_specs=[
            pl.BlockSpec((TM, 1, DX), lambda i, pos, zt: (jnp.minimum(i, n_p - 1), 0, 0)),
            pl.BlockSpec((TM, 1, DX), lambda i, pos, zt: (jnp.maximum(i - n_p, 0), 0, 0)),
        ],
        out_specs=pl.BlockSpec(memory_space=pl.ANY),
        scratch_shapes=[pltpu.VMEM((TMM, 1, DX), jnp.float32),
                        pltpu.SemaphoreType.DMA, pltpu.SemaphoreType.DMA],
    )
    return pl.pallas_call(
        functools.partial(_dispatch_kernel, n_prompt_tiles=n_p, n_sorted_tiles=n_sorted_tiles),
        out_shape=jax.ShapeDtypeStruct((n_sorted_tiles * TMM, 1, DX), jnp.float32),
        grid_spec=grid_spec,
        compiler_params=pltpu.CompilerParams(
            dimension_semantics=("arbitrary",), vmem_limit_bytes=VMEM_LIMIT),
        name="dispatch",
    )(pos, zero_tiles, h2_p, h2_s)


def _split_into(w_ref, s_ref):
    n_rows = w_ref.shape[0]
    for c in range(0, n_rows, SPLIT_ROWS):
        hi, lo = _split(w_ref[c:c + SPLIT_ROWS, :])
        s_ref[0, c:c + SPLIT_ROWS, :] = hi
        s_ref[1, c:c + SPLIT_ROWS, :] = lo


def _expert_kernel(ea_ref, eb_ref, nu_ref, xs_ref, fga_ref, fua_ref, fda_ref, fgb_ref, fub_ref,
                   fdb_ref, ys_ref, x2d_ref, wga_ref, wua_ref, wda_ref, wgb_ref, wub_ref, wdb_ref):
    j = pl.program_id(0)
    prev = jnp.maximum(j - 1, 0)

    @pl.when((j == 0) | (ea_ref[j] != ea_ref[prev]))
    def _():
        _split_into(fga_ref, wga_ref)
        _split_into(fua_ref, wua_ref)
        _split_into(fda_ref, wda_ref)

    @pl.when((j == 0) | (eb_ref[j] != eb_ref[prev]))
    def _():
        _split_into(fgb_ref, wgb_ref)
        _split_into(fub_ref, wub_ref)
        _split_into(fdb_ref, wdb_ref)

    @pl.when(j < nu_ref[0])
    def _():
        x2d_ref[...] = xs_ref[...].reshape(TMM, DX)
        x_hi, x_lo = _split(x2d_ref[:, 0:D])
        gate_a = x2d_ref[:, D:D + 1]
        gate_b = x2d_ref[:, D + 1:D + 2]

        def expert(wg_ref, wu_ref, wd_ref, gate):
            hg = _dot3(x_hi, x_lo, wg_ref[0], wg_ref[1])
            hu = _dot3(x_hi, x_lo, wu_ref[0], wu_ref[1])
            hid_hi, hid_lo = _split(hg * jax.nn.sigmoid(hg) * hu * gate)
            return _dot3(hid_hi, hid_lo, wd_ref[0], wd_ref[1])

        y = expert(wga_ref, wua_ref, wda_ref, gate_a) + expert(wgb_ref, wub_ref, wdb_ref, gate_b)
        ys_ref[...] = y.reshape(TMM, 1, D)

    @pl.when(j >= nu_ref[0])
    def _():
        ys_ref[...] = jnp.zeros_like(ys_ref)


def _experts(ea, eb, n_used, xs, wg, wu, wd, *, layer):
    n_tiles = xs.shape[0] // TMM

    def x_map(j, ea, eb, nu):
        return (jnp.minimum(j, nu[0] - 1), 0, 0)

    def a_map(j, ea, eb, nu):
        return (layer, ea[j], 0, 0)

    def b_map(j, ea, eb, nu):
        return (layer, eb[j], 0, 0)

    up = (None, None, D, D_FF)
    down = (None, None, D_FF, D)
    up_split = pltpu.VMEM((2, D, D_FF), jnp.bfloat16)
    down_split = pltpu.VMEM((2, D_FF, D), jnp.bfloat16)
    grid_spec = pltpu.PrefetchScalarGridSpec(
        num_scalar_prefetch=3,
        grid=(n_tiles,),
        in_specs=[
            pl.BlockSpec((TMM, 1, DX), x_map),
            pl.BlockSpec(up, a_map), pl.BlockSpec(up, a_map), pl.BlockSpec(down, a_map),
            pl.BlockSpec(up, b_map), pl.BlockSpec(up, b_map), pl.BlockSpec(down, b_map),
        ],
        out_specs=pl.BlockSpec((TMM, 1, D), lambda j, ea, eb, nu: (j, 0, 0)),
        scratch_shapes=[pltpu.VMEM((TMM, DX), jnp.float32),
                        up_split, up_split, down_split, up_split, up_split, down_split],
    )
    return pl.pallas_call(
        _expert_kernel,
        out_shape=jax.ShapeDtypeStruct((xs.shape[0], 1, D), jnp.float32),
        grid_spec=grid_spec,
        compiler_params=pltpu.CompilerParams(
            dimension_semantics=("arbitrary",), vmem_limit_bytes=VMEM_LIMIT),
        name="experts",
    )(ea, eb, n_used, xs, wg, wu, wd, wg, wu, wd)


def _combine_kernel(pos_ref, xp_ref, xs_ref, gp_ref, gs_ref, fg_ref, ys_ref, op_ref, os_ref,
                    g3_ref, g2_ref, sem, *, n_prompt_tiles, final):
    i = pl.program_id(0)

    def body(c, carry):
        for u in range(ROW_UNROLL):
            r = c * ROW_UNROLL + u
            p = pos_ref[i * TM + r]
            pltpu.make_async_copy(ys_ref.at[p], g3_ref.at[r], sem).start(priority=u % 2)
        return carry
    lax.fori_loop(0, TM // ROW_UNROLL, body, 0)
    _row_copy_wait(ys_ref.at[pl.ds(0, TM)], g3_ref, sem)
    g2_ref[...] = g3_ref[...].reshape(TM, D)

    def finish(x_ref, gate_ref, o_ref):
        x2 = _gated_add(x_ref[...], gate_ref[...], g2_ref[...])
        if final:
            ms = jnp.mean(x2 * x2, axis=-1, keepdims=True)
            x2 = x2 * lax.rsqrt(ms + EPS) * fg_ref[...]
        o_ref[...] = x2

    @pl.when(i < n_prompt_tiles)
    def _():
        finish(xp_ref, gp_ref, op_ref)

    @pl.when(i >= n_prompt_tiles)
    def _():
        finish(xs_ref, gs_ref, os_ref)


def _combine(pos, x1_p, x1_s, mod, final_g, ys, *, layer, final):
    n_p = x1_p.shape[0] // TM
    n_s = x1_s.shape[0] // TM

    def p_map(i, pos):
        return (jnp.minimum(i, n_p - 1), 0)

    def s_map(i, pos):
        return (jnp.maximum(i - n_p, 0), 0)

    grid_spec = pltpu.PrefetchScalarGridSpec(
        num_scalar_prefetch=1,
        grid=(n_p + n_s,),
        in_specs=[
            pl.BlockSpec((TM, D), p_map),
            pl.BlockSpec((TM, D), s_map),
            pl.BlockSpec((None, None, MOD_ROWS, D), lambda i, pos: (layer, 5, 0, 0)),
            pl.BlockSpec((None, None, MOD_ROWS, D), lambda i, pos: (layer, 5, 1, 0)),
            pl.BlockSpec((1, D), lambda i, pos: (0, 0)),
            pl.BlockSpec(memory_space=pl.ANY),
        ],
        out_specs=[pl.BlockSpec((TM, D), p_map), pl.BlockSpec((TM, D), s_map)],
        scratch_shapes=[
            pltpu.VMEM((TM, 1, D), jnp.float32),
            pltpu.VMEM((TM, D), jnp.float32),
            pltpu.SemaphoreType.DMA,
        ],
    )
    return pl.pallas_call(
        functools.partial(_combine_kernel, n_prompt_tiles=n_p, final=final),
        out_shape=[jax.ShapeDtypeStruct(x1_p.shape, jnp.float32),
                   jax.ShapeDtypeStruct(x1_s.shape, jnp.float32)],
        grid_spec=grid_spec,
        compiler_params=pltpu.CompilerParams(
            dimension_semantics=("arbitrary",), vmem_limit_bytes=VMEM_LIMIT),
        name="combine",
    )(pos, x1_p, x1_s, mod, mod, final_g, ys)


def _lookup(table, idx):
    k = jnp.arange(table.shape[0], dtype=jnp.int32)
    return jnp.sum(jnp.where(idx[:, None] == k[None, :], table[None, :], 0), axis=1)


def _plan(route_p, route_s, cnt_p, cnt_s, n_sorted_tiles):
    b_p = route_p[:, 0, :].reshape(-1).astype(jnp.int32)
    r_p = route_p[:, 1, :].reshape(-1).astype(jnp.int32)
    b_s = route_s[:, 0, :].reshape(-1).astype(jnp.int32)
    r_s = route_s[:, 1, :].reshape(-1).astype(jnp.int32)
    c_p = cnt_p[:, 0].astype(jnp.int32)
    c_s = cnt_s[:, 0].astype(jnp.int32)
    tiles_per_bucket = (c_p + c_s + TMM - 1) // TMM
    tile_end = jnp.cumsum(tiles_per_bucket)
    row_start = (tile_end - tiles_per_bucket) * TMM
    pos = jnp.concatenate([_lookup(row_start, b_p) + r_p, _lookup(row_start + c_p, b_s) + r_s])
    n_used = tile_end[-1]
    tile = jnp.minimum(jnp.arange(n_sorted_tiles, dtype=jnp.int32), n_used - 1)
    tile_bucket = jnp.sum((tile[:, None] >= tile_end[None, :]).astype(jnp.int32), axis=1)
    group = tile_bucket // 6
    pair = tile_bucket % 6
    ea = GROUP_SIZE * group + _lookup(jnp.asarray(_PAIR_A, jnp.int32), pair)
    eb = GROUP_SIZE * group + _lookup(jnp.asarray(_PAIR_B, jnp.int32), pair)
    all_tiles = jnp.arange(n_sorted_tiles, dtype=jnp.int32)
    last_of_bucket = jnp.any((all_tiles[:, None] == tile_end[None, :] - 1)
                             & (tiles_per_bucket[None, :] > 0), axis=1)
    zero_tiles = (last_of_bucket | (all_tiles >= n_used)).astype(jnp.int32)
    return pos, ea, eb, n_used.reshape(1), zero_tiles


def _time_major(x):
    b, s, d = x.shape
    return jnp.transpose(x, (1, 0, 2)).reshape(s * b, d)


def _batch_major(x, batch):
    rows, d = x.shape
    return jnp.transpose(x.reshape(rows // batch, batch, d), (1, 0, 2))


def kernel(x_prompt, x_sample, c_prompt, c_sample, state_pool_buf, state_conv_buf, state_lru_h,
           w_mod, b_mod, norm_g, pool_w, pool_scale, lru_w_x, lru_b_x, lru_w_y, lru_b_y,
           conv_w, conv_b, lru_w_a, lru_b_a, lru_w_i, lru_b_i, lru_lambda, lru_w_out, lru_b_out,
           router_w, router_b, moe_w_gate, moe_w_up, moe_w_down, final_g):
    bp, sp, _ = x_prompt.shape
    bs, ss, _ = x_sample.shape
    f32 = jnp.float32

    c_exp = jnp.concatenate([jnp.tile(c_prompt, (MOD_ROWS // bp, 1)),
                             jnp.tile(c_sample, (MOD_ROWS // bs, 1))], axis=0)
    mod = _modulation(c_exp, w_mod, b_mod)
    norm_g4 = norm_g.reshape(DEPTH, 2, 1, D)
    rw_t = router_w.T
    rb = router_b.reshape(N_EXPERTS, 1)
    fg = final_g.reshape(1, D)

    xp = _time_major(x_prompt)
    xs = _time_major(x_sample)
    n_tokens = xp.shape[0] + xs.shape[0]
    n_sorted_tiles = n_tokens // TMM + N_BUCKETS

    groups = (
        dict(batch=bp, start=0, group=0),
        dict(batch=bs, start=PAST_LEN, group=1),
    )
    new_pool = ([], [])
    new_conv = ([], [])
    new_h = ([], [])
    x = [xp, xs]
    for layer in range(DEPTH):
        j = layer // 2
        outs = []
        for gi, g in enumerate(groups):
            b = g["batch"]
            if layer % 2 == 0:
                if gi == 0:
                    buf = jnp.zeros((POOL_BUF * b, D), f32)
                else:
                    buf = _time_major(state_pool_buf[j])
                x1, h2, route, cnt, nbuf = _pool_layer(
                    x[gi], mod, norm_g4, buf, _split_weight(pool_w[j]), pool_scale[j].reshape(1, D),
                    rw_t, rb, layer=layer, **g)
                new_pool[gi].append(_batch_major(nbuf, b))
            else:
                if gi == 0:
                    cbuf = jnp.zeros(((CONV_WIDTH - 1) * b, D), f32)
                    h0 = jnp.zeros((b, D), f32)
                else:
                    cbuf = _time_major(state_conv_buf[j])
                    h0 = state_lru_h[j]
                p = dict(
                    w_x=_split_weight(lru_w_x[j]), b_x=lru_b_x[j].reshape(1, D),
                    w_y=_split_weight(lru_w_y[j]), b_y=lru_b_y[j].reshape(1, D),
                    cw=conv_w[j], cb=conv_b[j].reshape(1, D),
                    w_a=_split_weight(lru_w_a[j]), b_a=lru_b_a[j].reshape(1, D),
                    w_i=_split_weight(lru_w_i[j]), b_i=lru_b_i[j].reshape(1, D),
                    lam=lru_lambda[j].reshape(1, D),
                    w_out=_split_weight(lru_w_out[j]), b_out=lru_b_out[j].reshape(1, D))
                x1, h2, route, cnt, nconv, nh = _lru_layer(
                    x[gi], mod, norm_g4, cbuf, h0, p, rw_t, rb, layer=layer, **g)
                new_conv[gi].append(_batch_major(nconv, b))
                new_h[gi].append(nh)
            outs.append((x1, h2, route, cnt))
        (x1_p, h2_p, route_p, cnt_p), (x1_s, h2_s, route_s, cnt_s) = outs
        pos, ea, eb, n_used, zero_tiles = _plan(route_p, route_s, cnt_p, cnt_s, n_sorted_tiles)
        xsorted = _dispatch(pos, zero_tiles, h2_p, h2_s)
        ysorted = _experts(ea, eb, n_used, xsorted, moe_w_gate, moe_w_up, moe_w_down, layer=layer)
        x = list(_combine(pos, x1_p, x1_s, mod, fg, ysorted, layer=layer,
                          final=(layer == DEPTH - 1)))

    y_prompt = _batch_major(x[0], bp)
    y_sample = _batch_major(x[1], bs)
    return (y_prompt, y_sample,
            jnp.stack(new_pool[0]), jnp.stack(new_pool[1]),
            jnp.stack(new_conv[0]), jnp.stack(new_conv[1]),
            jnp.stack(new_h[0]), jnp.stack(new_h[1]))
```
